```python
import jax, jax.numpy as jnp
from jax import lax
import numpy as np

D_MODEL = 1024
BATCH = 2
SEQ = 8192
DEPTH = 4

HEAD_DIM = 64
NORM_EPS = 1e-6
GM_HEADS = 4
GM_WIDTH = GM_HEADS * HEAD_DIM
GM_CHUNK = 128
GLA_HEADS = 4
GLA_DV = HEAD_DIM
GLA_DK = HEAD_DIM // 2
GLA_WIDTH = GLA_HEADS * GLA_DV
GLA_KEY_WIDTH = GLA_HEADS * GLA_DK
GLA_GATE_RANK = 16
GLA_TAU = 16.0
GLA_CHUNK = 16
SWA_Q_HEADS = 8
SWA_KV_HEADS = 2
SWA_GROUP = SWA_Q_HEADS // SWA_KV_HEADS
SWA_WIDTH = SWA_Q_HEADS * HEAD_DIM
SWA_KV_WIDTH = SWA_KV_HEADS * HEAD_DIM
SWA_WINDOW = 128
SWA_BLOCK = 128
ROPE_THETA = 10000.0
D_MIX = GM_WIDTH + GLA_WIDTH + SWA_WIDTH
IN_SPLITS = (GM_WIDTH, GM_WIDTH, GM_WIDTH,
             GLA_KEY_WIDTH, GLA_KEY_WIDTH, GLA_WIDTH,
             GLA_GATE_RANK, GLA_WIDTH,
             SWA_WIDTH, SWA_KV_WIDTH, SWA_KV_WIDTH, SWA_WIDTH)
D_IN = 3 * GM_WIDTH + 2 * GLA_KEY_WIDTH + 2 * GLA_WIDTH + GLA_GATE_RANK + 2 * SWA_WIDTH + 2 * SWA_KV_WIDTH

kernel_name = "hybrid_sgu_gla_swa_sink_block"


def rms_norm(x, g):
    xf = x.astype(jnp.float32)
    y = xf * lax.rsqrt(jnp.mean(xf * xf, axis=-1, keepdims=True) + NORM_EPS)
    return (y * g.astype(jnp.float32)).astype(x.dtype)


def layer_norm(x, g, b):
    xf = x.astype(jnp.float32)
    mu = jnp.mean(xf, axis=-1, keepdims=True)
    xc = xf - mu
    y = xc * lax.rsqrt(jnp.mean(xc * xc, axis=-1, keepdims=True) + NORM_EPS)
    return (y * g.astype(jnp.float32) + b.astype(jnp.float32)).astype(x.dtype)


def rope(x, positions):
    d = x.shape[-1]
    half = d // 2
    inv_freq = jnp.power(ROPE_THETA, -jnp.arange(half, dtype=jnp.float32) * 2.0 / d)
    ang = positions.astype(jnp.float32)[..., None] * inv_freq
    cos = jnp.cos(ang)[:, :, None, :]
    sin = jnp.sin(ang)[:, :, None, :]
    xf = x.astype(jnp.float32)
    x1, x2 = xf[..., :half], xf[..., half:]
    return jnp.concatenate([x1 * cos - x2 * sin, x2 * cos + x1 * sin], axis=-1).astype(x.dtype)


def spatial_gating(u, v, ln_g, ln_b, ws, bs):
    b, s, _ = u.shape
    nc = s // GM_CHUNK
    v = layer_norm(v, ln_g, ln_b)
    vc = v.reshape(b, nc, GM_CHUNK, GM_HEADS, HEAD_DIM)
    causal = jnp.tril(jnp.ones((GM_CHUNK, GM_CHUNK), dtype=bool))
    w = jnp.where(causal[None], ws, 0.0).astype(vc.dtype)
    sv = jnp.einsum('htc,bnchd->bnthd', w, vc) + bs.T.astype(vc.dtype)[:, :, None]
    return u * sv.reshape(b, s, GM_WIDTH)


def gated_linear_attention(q, k, v, g_low, wg2, bg, norm_g):
    b, s, _ = q.shape
    nc, L = s // GLA_CHUNK, GLA_CHUNK
    f32 = jnp.float32
    q = q.astype(f32).reshape(b, nc, L, GLA_HEADS, GLA_DK) * (GLA_DK ** -0.5)
    k = k.astype(f32).reshape(b, nc, L, GLA_HEADS, GLA_DK)
    v = v.astype(f32).reshape(b, nc, L, GLA_HEADS, GLA_DV)
    logit = jnp.einsum('bsr,rc->bsc', g_low.astype(f32), wg2.astype(f32)) + bg.astype(f32)
    log_a = jax.nn.log_sigmoid(logit) / GLA_TAU
    G = jnp.cumsum(log_a.reshape(b, nc, L, GLA_HEADS, GLA_DK), axis=2)
    causal = jnp.tril(jnp.ones((L, L), dtype=bool))[:, :, None, None]
    diff = G[:, :, :, None] - G[:, :, None, :]
    decay = jnp.where(causal, jnp.exp(jnp.where(causal, diff, 0.0)), 0.0)
    attn = jnp.einsum('bnthk,bnshk,bntshk->bnhts', q, k, decay)
    o_intra = jnp.einsum('bnhts,bnshv->bnthv', attn, v)
    G_last = G[:, :, -1]
    k_dec = k * jnp.exp(G_last[:, :, None] - G)
    U = jnp.einsum('bnshk,bnshv->bnhkv', k_dec, v)
    chunk_decay = jnp.exp(G_last)

    def step(state, inp):
        a, u_n = inp
        return a[..., None] * state + u_n, state

    s0 = jnp.zeros((b, GLA_HEADS, GLA_DK, GLA_DV), f32)
    _, s_prev = lax.scan(step, s0, (jnp.moveaxis(chunk_decay, 1, 0), jnp.moveaxis(U, 1, 0)))
    s_prev = jnp.moveaxis(s_prev, 0, 1)
    o_inter = jnp.einsum('bnthk,bnhkv->bnthv', q * jnp.exp(G), s_prev)
    o = (o_intra + o_inter).reshape(b, s, GLA_HEADS, GLA_DV)
    o = rms_norm(o, norm_g)
    return o.reshape(b, s, GLA_WIDTH)


def sliding_window_attention(q, k, v, sinks, positions):
    b, s, _ = q.shape
    T = SWA_BLOCK
    nb = s // T
    q = rope(q.reshape(b, s, SWA_Q_HEADS, HEAD_DIM), positions)
    k = rope(k.reshape(b, s, SWA_KV_HEADS, HEAD_DIM), positions)
    v = v.reshape(b, s, SWA_KV_HEADS, HEAD_DIM)
    qb = q.reshape(b, nb, T, SWA_KV_HEADS, SWA_GROUP, HEAD_DIM)

    def band(t):
        tp = jnp.pad(t, ((0, 0), (T, 0), (0, 0), (0, 0))).reshape(b, nb + 1, T, SWA_KV_HEADS, HEAD_DIM)
        return jnp.concatenate([tp[:, :-1], tp[:, 1:]], axis=2)

    kb, vb = band(k), band(v)
    scores = jnp.einsum('bnqhgd,bnkhd->bnhgqk', qb, kb).astype(jnp.float32) * (HEAD_DIM ** -0.5)
    qpos = jnp.arange(nb)[:, None] * T + jnp.arange(T)[None, :]
    kpos = (jnp.arange(nb)[:, None] - 1) * T + jnp.arange(2 * T)[None, :]
    rel = qpos[:, :, None] - kpos[:, None, :]
    mask = (rel >= 0) & (rel < SWA_WINDOW) & (kpos[:, None, :] >= 0)
    scores = jnp.where(mask[None, :, None, None], scores, -jnp.inf)
    sink = jnp.broadcast_to(sinks.astype(jnp.float32).reshape(1, 1, SWA_KV_HEADS, SWA_GROUP, 1, 1),
                            scores.shape[:-1] + (1,))
    probs = jax.nn.softmax(jnp.concatenate([scores, sink], axis=-1), axis=-1)[..., :-1]
    out = jnp.einsum('bnhgqk,bnkhd->bnqhgd', probs.astype(vb.dtype), vb)
    return out.reshape(b, s, SWA_WIDTH)


def hybrid_layer(x, positions, pre_g, w_in, gm_ln_g, gm_ln_b, gm_ws, gm_bs,
                 gla_wg2, gla_bg, gla_norm_g, swa_sinks, w_out, post_g):
    h = rms_norm(x, pre_g)
    proj = jnp.einsum('bsd,dc->bsc', h, w_in)
    split_points = np.cumsum(IN_SPLITS)[:-1].tolist()
    (gm_u, gm_v, gm_z, gla_q, gla_k, gla_v, gla_glr, gla_z,
     swa_q, swa_k, swa_v, swa_z) = jnp.split(proj, split_points, axis=-1)
    y_a = spatial_gating(gm_u, gm_v, gm_ln_g, gm_ln_b, gm_ws, gm_bs) * jax.nn.silu(gm_z)
    y_b = gated_linear_attention(gla_q, gla_k, gla_v, gla_glr, gla_wg2, gla_bg,
                                 gla_norm_g).astype(x.dtype) * jax.nn.silu(gla_z)
    y_c = sliding_window_attention(swa_q, swa_k, swa_v, swa_sinks, positions) * jax.nn.silu(swa_z)
    y = jnp.einsum('bsc,cd->bsd', jnp.concatenate([y_a, y_b, y_c], axis=-1), w_out)
    return x + rms_norm(y, post_g)


def setup_inputs(seed: int = 0) -> dict:
    key = jax.random.key(seed)
    ks = jax.random.split(key, 16)
    f32 = jnp.float32
    x = jax.random.normal(ks[0], (BATCH, SEQ, D_MODEL), f32)
    positions = jnp.broadcast_to(jnp.arange(SEQ, dtype=jnp.int32)[None, :], (BATCH, SEQ))
    pre_g = 1.0 + 0.1 * jax.random.normal(ks[1], (DEPTH, D_MODEL), f32)
    w_in = jax.random.normal(ks[2], (DEPTH, D_MODEL, D_IN), f32) * (D_MODEL ** -0.5)
    gm_ln_g = 1.0 + 0.1 * jax.random.normal(ks[3], (DEPTH, GM_WIDTH), f32)
    gm_ln_b = 0.02 * jax.random.normal(ks[4], (DEPTH, GM_WIDTH), f32)
    gm_ws = jax.random.normal(ks[5], (DEPTH, GM_HEADS, GM_CHUNK, GM_CHUNK), f32) * (GM_CHUNK ** -0.5)
    gm_bs = 1.0 + 0.1 * jax.random.normal(ks[6], (DEPTH, GM_HEADS, GM_CHUNK), f32)
    gla_wg2 = jax.random.normal(ks[7], (DEPTH, GLA_GATE_RANK, GLA_KEY_WIDTH), f32) * (GLA_GATE_RANK ** -0.5)
    gla_bg = 0.1 * jax.random.normal(ks[8], (DEPTH, GLA_KEY_WIDTH), f32)
    gla_norm_g = 1.0 + 0.1 * jax.random.normal(ks[9], (DEPTH, GLA_DV), f32)
    swa_sinks = 0.5 * jax.random.normal(ks[10], (DEPTH, SWA_Q_HEADS), f32)
    w_out = jax.random.normal(ks[11], (DEPTH, D_MIX, D_MODEL), f32) * (D_MIX ** -0.5)
    post_g = 1.0 + 0.1 * jax.random.normal(ks[12], (DEPTH, D_MODEL), f32)
    return {"x": x, "positions": positions, "pre_g": pre_g, "w_in": w_in,
            "gm_ln_g": gm_ln_g, "gm_ln_b": gm_ln_b, "gm_ws": gm_ws, "gm_bs": gm_bs,
            "gla_wg2": gla_wg2, "gla_bg": gla_bg, "gla_norm_g": gla_norm_g,
            "swa_sinks": swa_sinks, "w_out": w_out, "post_g": post_g}


def reference(x, positions, pre_g, w_in, gm_ln_g, gm_ln_b, gm_ws, gm_bs,
              gla_wg2, gla_bg, gla_norm_g, swa_sinks, w_out, post_g):
    h = x
    for l in range(DEPTH):
        h = hybrid_layer(h, positions, pre_g[l], w_in[l], gm_ln_g[l], gm_ln_b[l], gm_ws[l], gm_bs[l],
                         gla_wg2[l], gla_bg[l], gla_norm_g[l], swa_sinks[l], w_out[l], post_g[l])
    return h
```

```python
import functools

import numpy as np
import jax
import jax.numpy as jnp
from jax import lax
from jax.experimental import pallas as pl
from jax.experimental.pallas import tpu as pltpu

F32 = jnp.float32
BF16 = jnp.bfloat16

D_MODEL = 1024
DEPTH = 4
HEAD_DIM = 64
NORM_EPS = 1e-6
GM_HEADS = 4
GM_WIDTH = 256
GLA_HEADS = 4
GLA_DV = 64
GLA_DK = 32
GLA_WIDTH = 256
GLA_KEY_WIDTH = 128
GLA_GATE_RANK = 16
GLA_TAU = 16.0
GLA_CHUNK = 16
SWA_Q_HEADS = 8
SWA_KV_HEADS = 2
SWA_WIDTH = 512
SWA_KV_WIDTH = 128
ROPE_THETA = 10000.0
D_MIX = 1024
D_IN = 2832

BLK = 128
LANES = 128
TOKEN_BLOCK = 512
VMEM_LIMIT_BYTES = 56 * 1024 * 1024

C_GM_U, C_GM_V, C_GM_Z = 0, 256, 512
C_GLA_Q, C_GLA_K, C_GLA_V, C_GLA_Z = 768, 896, 1024, 1280
C_SWA_Q, C_SWA_K, C_SWA_V, C_SWA_Z = 1536, 2048, 2304, 2432
C_GLA_GLR = 2944
D_IN_PAD = 3072
Y_A, Y_B, Y_C = 0, 256, 512


def _in_proj_columns():
    o = np.cumsum((0, 256, 256, 256, 128, 128, 256, 16, 256, 512, 128, 128, 512))
    (o_u, o_v, o_z, o_q, o_k, o_gv, o_glr, o_gz, o_sq, o_sk, o_sv, o_sz, _) = o
    half = HEAD_DIM // 2
    r = np.arange
    cols = [o_u + r(256), o_v + r(256), o_z + r(256),
            o_q + r(128), o_k + r(128), o_gv + r(256), o_gz + r(256)]
    for p in range(SWA_Q_HEADS // 2):
        a, b = o_sq + 64 * (2 * p), o_sq + 64 * (2 * p + 1)
        cols += [a + r(half), b + r(half), a + half + r(half), b + half + r(half)]
    for g in range(SWA_KV_HEADS):
        a = o_sk + 64 * g
        cols += [a + r(half), a + r(half), a + half + r(half), a + half + r(half)]
    cols += [o_sv + r(128)]
    cols += [_swa_out_order() + o_sz]
    cols += [o_glr + r(GLA_GATE_RANK), np.full(LANES - GLA_GATE_RANK, -1)]
    cols = np.concatenate(cols)
    assert cols.shape == (D_IN_PAD,)
    return cols


def _swa_out_order():
    r = np.arange(HEAD_DIM)
    group = SWA_Q_HEADS // SWA_KV_HEADS
    return np.concatenate([np.concatenate([64 * c + r, 64 * (group + c) + r]) for c in range(group)])


def _silu(z):
    return z / (1.0 + jnp.exp(-z))


def _dot(a, b):
    return jnp.dot(a, b, preferred_element_type=F32)


def _dot_nt(a, b):
    return lax.dot_general(a, b, (((1,), (1,)), ((), ())), preferred_element_type=F32)


def _split_dot(m, x):
    hi = x.astype(BF16)
    lo = (x - hi.astype(F32)).astype(BF16)
    return _dot(m, hi) + _dot(m, lo)


def _rope_table_kernel(pos_ref, invf_ref, cos_ref, sin_ref):
    ang = pos_ref[...] * invf_ref[...]
    lane = lax.broadcasted_iota(jnp.int32, ang.shape, 1)
    cos_ref[...] = jnp.cos(ang)
    sin_ref[...] = jnp.where(lane < LANES // 2, -jnp.sin(ang), jnp.sin(ang))


def _layer_kernel(sinks_ref, x_ref, cos_ref, sin_ref, pre_g_ref, w_in_ref, lng_ref, lnb_ref,
                  ws_ref, bs_ref, wg2_ref, bg_ref, gng_ref, w_out_ref, post_g_ref,
                  o_ref, proj_ref, y_ref, state_ref, kprev_ref, vprev_ref, *, n_sub):
    step = pl.program_id(1)

    @pl.when(step == 0)
    def _():
        state_ref[...] = jnp.zeros_like(state_ref)
        kprev_ref[...] = jnp.zeros_like(kprev_ref)
        vprev_ref[...] = jnp.zeros_like(vprev_ref)

    x = x_ref[...]
    h = x * lax.rsqrt(jnp.mean(x * x, axis=-1, keepdims=True) + NORM_EPS) * pre_g_ref[...]
    proj_ref[...] = _dot(h.astype(BF16), w_in_ref[...])

    def sub_block(i, carry):
        rows = pl.ds(pl.multiple_of(i * BLK, BLK), BLK)
        row = lax.broadcasted_iota(jnp.int32, (BLK, BLK), 0)
        col = lax.broadcasted_iota(jnp.int32, (BLK, BLK), 1)

        u = proj_ref[rows, C_GM_U:C_GM_U + GM_WIDTH]
        v = proj_ref[rows, C_GM_V:C_GM_V + GM_WIDTH]
        mu = jnp.mean(v, axis=-1, keepdims=True)
        vc = v - mu
        var = jnp.mean(vc * vc, axis=-1, keepdims=True)
        vln = (vc * lax.rsqrt(var + NORM_EPS) * lng_ref[...] + lnb_ref[...]).astype(BF16)
        lane_head = lax.broadcasted_iota(jnp.int32, (BLK, GM_WIDTH), 1) // HEAD_DIM
        sv = bs_ref[...]
        for hh in range(GM_HEADS):
            w = jnp.where(col <= row, ws_ref[hh], 0.0).astype(BF16)
            sv = sv + jnp.where(lane_head == hh, _dot(w, vln), 0.0)
        ya = u * sv * _silu(proj_ref[rows, C_GM_Z:C_GM_Z + GM_WIDTH])
        y_ref[rows, Y_A:Y_A + GM_WIDTH] = ya.astype(BF16)

        q = proj_ref[rows, C_GLA_Q:C_GLA_Q + GLA_KEY_WIDTH] * (GLA_DK ** -0.5)
        k = proj_ref[rows, C_GLA_K:C_GLA_K + GLA_KEY_WIDTH]
        gv = proj_ref[rows, C_GLA_V:C_GLA_V + GLA_WIDTH]
        glr = proj_ref[rows, C_GLA_GLR:C_GLA_GLR + LANES].astype(BF16)
        logit = _dot(glr, wg2_ref[...]) + bg_ref[...]
        log_a = (jnp.minimum(logit, 0.0) - jnp.log1p(jnp.exp(-jnp.abs(logit)))) * (1.0 / GLA_TAU)
        same_chunk = (row // GLA_CHUNK) == (col // GLA_CHUNK)
        tri16 = jnp.where(same_chunk & (col <= row), 1.0, 0.0).astype(BF16)
        ones16 = jnp.where(same_chunk, 1.0, 0.0).astype(BF16)
        g_cum = _split_dot(tri16, log_a)
        g_end = _split_dot(ones16, log_a)
        q_dec = q * jnp.exp(g_cum)
        k_dec = k * jnp.exp(g_end - g_cum)
        a_chunk = jnp.exp(g_end)

        hk = lax.broadcasted_iota(jnp.int32, (GLA_KEY_WIDTH, GLA_WIDTH), 0) // GLA_DK
        hv = lax.broadcasted_iota(jnp.int32, (GLA_KEY_WIDTH, GLA_WIDTH), 1) // GLA_DV
        head_match = hk == hv
        expand = jnp.where(head_match, 1.0, 0.0).astype(BF16)
        n_chunk = BLK // GLA_CHUNK
        g3 = g_cum.reshape(n_chunk, GLA_CHUNK, GLA_KEY_WIDTH)
        k3 = k.reshape(n_chunk, GLA_CHUNK, GLA_KEY_WIDTH)
        v3 = gv.reshape(n_chunk, GLA_CHUNK, GLA_WIDTH)
        t_local = row % GLA_CHUNK
        o_gla = jnp.zeros((BLK, GLA_WIDTH), F32)
        for s in range(GLA_CHUNK):
            g_s = jnp.broadcast_to(g3[:, s:s + 1, :], g3.shape).reshape(BLK, GLA_KEY_WIDTH)
            k_s = jnp.broadcast_to(k3[:, s:s + 1, :], k3.shape).reshape(BLK, GLA_KEY_WIDTH)
            v_s = jnp.broadcast_to(v3[:, s:s + 1, :], v3.shape).reshape(BLK, GLA_WIDTH)
            term = jnp.where(t_local >= s, q * k_s * jnp.exp(g_cum - g_s), 0.0)
            o_gla = o_gla + _dot(term.astype(BF16), expand) * v_s

        k_dec_t = k_dec.T
        lhs = jnp.concatenate(
            [jnp.where(col // GLA_CHUNK == jj, k_dec_t, 0.0) for jj in range(n_chunk)], axis=0)
        upd = _dot(lhs.astype(BF16), gv.astype(BF16))
        a_t = a_chunk.T
        state = state_ref[...]
        outs = []
        for jj in range(n_chunk):
            qj = q_dec[jj * GLA_CHUNK:(jj + 1) * GLA_CHUNK].astype(BF16)
            outs.append(_dot(qj, state.astype(BF16)))
            a_j = a_t[:, jj * GLA_CHUNK:jj * GLA_CHUNK + 1]
            state = a_j * state + jnp.where(head_match, upd[jj * BLK:(jj + 1) * BLK], 0.0)
        state_ref[...] = state
        o_gla = o_gla + jnp.concatenate(outs, axis=0)
        seg_r = lax.broadcasted_iota(jnp.int32, (GLA_WIDTH, GLA_WIDTH), 0) // GLA_DV
        seg_c = lax.broadcasted_iota(jnp.int32, (GLA_WIDTH, GLA_WIDTH), 1) // GLA_DV
        seg = jnp.where(seg_r == seg_c, 1.0, 0.0).astype(BF16)
        ms = _split_dot_rhs(o_gla * o_gla, seg) * (1.0 / GLA_DV)
        yb = o_gla * lax.rsqrt(ms + NORM_EPS) * gng_ref[...]
        yb = yb * _silu(proj_ref[rows, C_GLA_Z:C_GLA_Z + GLA_WIDTH])
        y_ref[rows, Y_B:Y_B + GLA_WIDTH] = yb.astype(BF16)

        cosb = cos_ref[rows, :]
        sinb = sin_ref[rows, :]

        def rope(t):
            return t * cosb + pltpu.roll(t, LANES // 2, 1) * sinb

        k_cur = [rope(proj_ref[rows, C_SWA_K + LANES * g:C_SWA_K + LANES * (g + 1)]).astype(BF16)
                 for g in range(SWA_KV_HEADS)]
        v_cur = proj_ref[rows, C_SWA_V:C_SWA_V + SWA_KV_WIDTH].astype(BF16)
        k_all = [jnp.concatenate([kprev_ref[g], k_cur[g]], axis=0)
                 for g in range(SWA_KV_HEADS)]
        v_all = jnp.concatenate([vprev_ref[...], v_cur], axis=0)
        tq = lax.broadcasted_iota(jnp.int32, (BLK, 2 * BLK), 0)
        kc = lax.broadcasted_iota(jnp.int32, (BLK, 2 * BLK), 1)
        first_block = (step * n_sub + i) == 0
        prev_lo = jnp.where(first_block, BLK, 0)
        in_window = ((kc < BLK) & (kc > tq + prev_lo)) | ((kc >= BLK) & (kc - BLK <= tq))
        bias = jnp.where(in_window, 0.0, -jnp.inf)
        lane = lax.broadcasted_iota(jnp.int32, (BLK, LANES), 1)
        lane_parity = (lane // (HEAD_DIM // 2)) % 2
        heads = [None] * SWA_Q_HEADS
        for p in range(SWA_Q_HEADS // 2):
            qp = rope(proj_ref[rows, C_SWA_Q + LANES * p:C_SWA_Q + LANES * (p + 1)]) * (HEAD_DIM ** -0.5)
            g = p // (SWA_Q_HEADS // SWA_KV_HEADS // 2)
            for par in range(2):
                hh = 2 * p + par
                qm = jnp.where(lane_parity == par, qp, 0.0).astype(BF16)
                sc = _dot_nt(qm, k_all[g]) + bias
                sink = sinks_ref[hh]
                m = jnp.maximum(jnp.max(sc, axis=-1, keepdims=True), sink)
                e = jnp.exp(sc - m)
                denom = jnp.sum(e, axis=-1, keepdims=True) + jnp.exp(sink - m)
                heads[hh] = _dot(e.astype(BF16), v_all) / denom
        group = SWA_Q_HEADS // SWA_KV_HEADS
        for c in range(group):
            yc = jnp.where(lane < HEAD_DIM, heads[c], heads[group + c])
            yc = yc * _silu(proj_ref[rows, C_SWA_Z + LANES * c:C_SWA_Z + LANES * (c + 1)])
            y_ref[rows, Y_C + LANES * c:Y_C + LANES * (c + 1)] = yc.astype(BF16)
        for g in range(SWA_KV_HEADS):
            kprev_ref[g] = k_cur[g]
        vprev_ref[...] = v_cur
        return carry

    lax.fori_loop(0, n_sub, sub_block, 0)

    y2 = _dot(y_ref[...], w_out_ref[...])
    y2 = y2 * lax.rsqrt(jnp.mean(y2 * y2, axis=-1, keepdims=True) + NORM_EPS) * post_g_ref[...]
    o_ref[...] = x_ref[...] + y2


def _split_dot_rhs(x, m):
    hi = x.astype(BF16)
    lo = (x - hi.astype(F32)).astype(BF16)
    return _dot(hi, m) + _dot(lo, m)


def _rope_tables(positions):
    n = positions.size
    half = HEAD_DIM // 2
    inv_freq = jnp.power(ROPE_THETA, -jnp.arange(half, dtype=F32) * 2.0 / HEAD_DIM)
    invf = jnp.tile(inv_freq, LANES // half).reshape(1, LANES)
    pos = jnp.broadcast_to(positions.reshape(n, 1).astype(F32), (n, LANES))
    rows = 2048
    spec = pl.BlockSpec((rows, LANES), lambda i: (i, 0))
    return pl.pallas_call(
        _rope_table_kernel,
        grid=(n // rows,),
        in_specs=[spec, pl.BlockSpec((1, LANES), lambda i: (0, 0))],
        out_specs=[spec, spec],
        out_shape=[jax.ShapeDtypeStruct((n, LANES), F32)] * 2,
        name="rope_tables",
    )(pos, invf)


def _layer_call(batch, seq):
    tb = TOKEN_BLOCK
    n_sub = tb // BLK
    steps = seq // tb
    tok = lambda b, j: (b * steps + j, 0)
    const2 = lambda b, j: (0, 0)
    const3 = lambda b, j: (0, 0, 0)
    in_specs = [
        pl.BlockSpec(memory_space=pltpu.SMEM),
        pl.BlockSpec((tb, D_MODEL), tok),
        pl.BlockSpec((tb, LANES), tok),
        pl.BlockSpec((tb, LANES), tok),
        pl.BlockSpec((1, D_MODEL), const2),
        pl.BlockSpec((D_MODEL, D_IN_PAD), const2),
        pl.BlockSpec((1, GM_WIDTH), const2),
        pl.BlockSpec((1, GM_WIDTH), const2),
        pl.BlockSpec((GM_HEADS, BLK, BLK), const3),
        pl.BlockSpec((BLK, GM_WIDTH), const2),
        pl.BlockSpec((LANES, GLA_KEY_WIDTH), const2),
        pl.BlockSpec((1, GLA_KEY_WIDTH), const2),
        pl.BlockSpec((1, GLA_WIDTH), const2),
        pl.BlockSpec((D_MIX, D_MODEL), const2),
        pl.BlockSpec((1, D_MODEL), const2),
    ]
    return pl.pallas_call(
        functools.partial(_layer_kernel, n_sub=n_sub),
        grid=(batch, steps),
        in_specs=in_specs,
        out_specs=pl.BlockSpec((tb, D_MODEL), tok),
        out_shape=jax.ShapeDtypeStruct((batch * seq, D_MODEL), F32),
        scratch_shapes=[
            pltpu.VMEM((tb, D_IN_PAD), F32),
            pltpu.VMEM((tb, D_MIX), BF16),
            pltpu.VMEM((GLA_KEY_WIDTH, GLA_WIDTH), F32),
            pltpu.VMEM((SWA_KV_HEADS, BLK, LANES), BF16),
            pltpu.VMEM((BLK, SWA_KV_WIDTH), BF16),
        ],
        compiler_params=pltpu.CompilerParams(
            dimension_semantics=("arbitrary", "arbitrary"),
            vmem_limit_bytes=VMEM_LIMIT_BYTES),
        name="hybrid_layer",
    )


def kernel(x, positions, pre_g, w_in, gm_ln_g, gm_ln_b, gm_ws, gm_bs, gla_wg2, gla_bg, gla_norm_g,
           swa_sinks, w_out, post_g):
    batch, seq, d_model = x.shape
    assert d_model == D_MODEL and seq % TOKEN_BLOCK == 0
    assert w_in.shape == (DEPTH, D_MODEL, D_IN) and w_out.shape == (DEPTH, D_MIX, D_MODEL)

    cols = _in_proj_columns()
    w_in_p = jnp.where(cols >= 0, jnp.take(w_in, np.maximum(cols, 0), axis=2), 0.0).astype(BF16)
    out_rows = np.concatenate([np.arange(Y_C), Y_C + _swa_out_order()])
    w_out_p = jnp.take(w_out, out_rows, axis=1).astype(BF16)
    bs_p = jnp.repeat(jnp.swapaxes(gm_bs, 1, 2), HEAD_DIM, axis=2)
    wg2_p = jnp.pad(gla_wg2, ((0, 0), (0, LANES - GLA_GATE_RANK), (0, 0))).astype(BF16)
    gng_p = jnp.tile(gla_norm_g, (1, GLA_HEADS))

    cos_t, sin_t = _rope_tables(positions)
    layer = _layer_call(batch, seq)
    h = x.reshape(batch * seq, D_MODEL)
    for l in range(DEPTH):
        h = layer(swa_sinks[l], h, cos_t, sin_t, pre_g[l][None], w_in_p[l], gm_ln_g[l][None],
                  gm_ln_b[l][None], gm_ws[l], bs_p[l], wg2_p[l], gla_bg[l][None], gng_p[l][None],
                  w_out_p[l], post_g[l][None])
    return h.reshape(batch, seq, D_MODEL)
```

```python
import functools
import math

import numpy as np
import jax
import jax.numpy as jnp
from jax import lax
from jax.experimental import pallas as pl
from jax.experimental.pallas import tpu as pltpu

F32 = jnp.float32
BF16 = jnp.bfloat16

D_MODEL = 1024
DEPTH = 4
HEAD_DIM = 64
NORM_EPS = 1e-6
GM_HEADS = 4
GM_WIDTH = 256
GLA_HEADS = 4
GLA_DV = 64
GLA_DK = 32
GLA_WIDTH = 256
GLA_KEY_WIDTH = 128
GLA_GATE_RANK = 16
GLA_TAU = 16.0
GLA_CHUNK = 16
SWA_Q_HEADS = 8
SWA_KV_HEADS = 2
SWA_GROUP = SWA_Q_HEADS // SWA_KV_HEADS
SWA_WIDTH = 512
SWA_KV_WIDTH = 128
ROPE_THETA = 10000.0
D_MIX = 1024
D_IN = 2832
LOG2E = math.log2(math.e)

BLK = 128
LANES = 128
SUBLANES = 8
TOKEN_BLOCK = 512
VMEM_LIMIT_BYTES = 56 * 1024 * 1024

C_GM_U, C_GM_V, C_GM_Z = 0, 256, 512
C_GLA_Q, C_GLA_K, C_GLA_V, C_GLA_Z = 768, 896, 1024, 1280
C_SWA_Q, C_SWA_K, C_SWA_V, C_SWA_Z = 1536, 2048, 2304, 2432
C_GLA_GLR = 2944
D_IN_PAD = 3072
Y_A, Y_B, Y_C = 0, 256, 512


def _swa_out_order():
    r = np.arange(HEAD_DIM)
    return np.concatenate(
        [np.concatenate([64 * c + r, 64 * (SWA_GROUP + c) + r]) for c in range(SWA_GROUP)])


def _in_proj_columns():
    o = np.cumsum((0, 256, 256, 256, 128, 128, 256, 16, 256, 512, 128, 128, 512))
    (o_u, o_v, o_z, o_q, o_k, o_gv, o_glr, o_gz, o_sq, o_sk, o_sv, o_sz, _) = o
    half = HEAD_DIM // 2
    r = np.arange
    cols = [o_u + r(256), o_v + r(256), o_z + r(256),
            o_q + r(128), o_k + r(128), o_gv + r(256), o_gz + r(256)]
    for p in range(SWA_Q_HEADS // 2):
        a, b = o_sq + 64 * (2 * p), o_sq + 64 * (2 * p + 1)
        cols += [a + r(half), b + r(half), a + half + r(half), b + half + r(half)]
    for g in range(SWA_KV_HEADS):
        a = o_sk + 64 * g
        cols += [a + r(half), a + r(half), a + half + r(half), a + half + r(half)]
    cols += [o_sv + r(128)]
    cols += [_swa_out_order() + o_sz]
    cols += [o_glr + r(GLA_GATE_RANK), np.full(LANES - GLA_GATE_RANK, -1)]
    cols = np.concatenate(cols)
    assert cols.shape == (D_IN_PAD,)
    return cols


def _take_static(w, idx, axis):
    idx = np.asarray(idx)
    pieces, start = [], 0
    for end in range(1, len(idx) + 1):
        run_ends = end == len(idx) or (idx[end] != idx[end - 1] + 1) or idx[end] < 0 or idx[end - 1] < 0
        if not run_ends:
            continue
        n = end - start
        if idx[start] < 0:
            shape = list(w.shape)
            shape[axis] = n
            pieces.append(jnp.zeros(shape, w.dtype))
        else:
            pieces.append(lax.slice_in_dim(w, int(idx[start]), int(idx[start]) + n, axis=axis))
        start = end
    return jnp.concatenate(pieces, axis=axis)


def _silu(z):
    hz = 0.5 * z
    return hz + hz * jnp.tanh(hz)


def _dot(a, b):
    return jnp.dot(a, b, preferred_element_type=F32)


def _dot_nt(a, b):
    return lax.dot_general(a, b, (((1,), (1,)), ((), ())), preferred_element_type=F32)


def _split_dot(m, x):
    hi = x.astype(BF16)
    lo = (x - hi.astype(F32)).astype(BF16)
    return _dot(m, hi) + _dot(m, lo)


def _split_dot_rhs(x, m):
    hi = x.astype(BF16)
    lo = (x - hi.astype(F32)).astype(BF16)
    return _dot(hi, m) + _dot(lo, m)


def _rope_table_kernel(pos_ref, invf_ref, cos_ref, sin_ref):
    ang = pos_ref[...] * invf_ref[...]
    lane = lax.broadcasted_iota(jnp.int32, ang.shape, 1)
    cos_ref[...] = jnp.cos(ang)
    sin_ref[...] = jnp.where(lane < LANES // 2, -jnp.sin(ang), jnp.sin(ang))


def _layer_kernel(sinks_ref, x_ref, cos_ref, sin_ref, pre_g_ref, w_in_ref, lng_ref, lnb_ref,
                  ws_ref, bs_ref, wg2_ref, bg_ref, gng_ref, w_out_ref, post_g_ref,
                  o_ref, proj_ref, y_ref, state_ref, kprev_ref, vprev_ref,
                  gcum_ref, gk_ref, gv_ref, *, n_sub):
    step = pl.program_id(1)

    @pl.when(step == 0)
    def _():
        state_ref[...] = jnp.zeros_like(state_ref)
        kprev_ref[...] = jnp.zeros_like(kprev_ref)
        vprev_ref[...] = jnp.zeros_like(vprev_ref)

    x = x_ref[...]
    h = x * lax.rsqrt(jnp.mean(x * x, axis=-1, keepdims=True) + NORM_EPS) * pre_g_ref[...]
    proj_ref[...] = _dot(h.astype(BF16), w_in_ref[...])

    def sub_block(i, carry):
        rows = pl.ds(pl.multiple_of(i * BLK, BLK), BLK)
        row = lax.broadcasted_iota(jnp.int32, (BLK, BLK), 0)
        col = lax.broadcasted_iota(jnp.int32, (BLK, BLK), 1)

        def seg(c0, width):
            return proj_ref[rows, c0:c0 + width]

        u = seg(C_GM_U, GM_WIDTH)
        v = seg(C_GM_V, GM_WIDTH)
        mu = jnp.mean(v, axis=-1, keepdims=True)
        vc = v - mu
        var = jnp.mean(vc * vc, axis=-1, keepdims=True)
        vln = (vc * lax.rsqrt(var + NORM_EPS) * lng_ref[...] + lnb_ref[...]).astype(BF16)
        lane_head = lax.broadcasted_iota(jnp.int32, (BLK, GM_WIDTH), 1) // HEAD_DIM
        sv = _dot(jnp.where(col <= row, ws_ref[0], 0.0).astype(BF16), vln)
        for hh in range(1, GM_HEADS):
            w = jnp.where(col <= row, ws_ref[hh], 0.0).astype(BF16)
            sv = jnp.where(lane_head == hh, _dot(w, vln), sv)
        ya = u * (sv + bs_ref[...]) * _silu(seg(C_GM_Z, GM_WIDTH))
        y_ref[rows, Y_A:Y_A + GM_WIDTH] = ya.astype(BF16)

        q = seg(C_GLA_Q, GLA_KEY_WIDTH) * (GLA_DK ** -0.5)
        k = seg(C_GLA_K, GLA_KEY_WIDTH)
        gv = seg(C_GLA_V, GLA_WIDTH)
        glr = seg(C_GLA_GLR, LANES).astype(BF16)
        logit = _dot(glr, wg2_ref[...]) + bg_ref[...]
        log_a = (jnp.minimum(logit, 0.0) - jnp.log1p(jnp.exp(-jnp.abs(logit)))) * (LOG2E / GLA_TAU)
        same_chunk = (row // GLA_CHUNK) == (col // GLA_CHUNK)
        tri16 = jnp.where(same_chunk & (col <= row), 1.0, 0.0).astype(BF16)
        ones16 = jnp.where(same_chunk, 1.0, 0.0).astype(BF16)
        g_cum = _split_dot(tri16, log_a)
        g_end = _split_dot(ones16, log_a)
        q_dec = q * jnp.exp2(g_cum)
        k_dec = k * jnp.exp2(g_end - g_cum)
        a_chunk = jnp.exp2(g_end)
        gcum_ref[...] = g_cum
        gk_ref[...] = k
        gv_ref[...] = gv

        hk = lax.broadcasted_iota(jnp.int32, (GLA_KEY_WIDTH, GLA_WIDTH), 0) // GLA_DK
        hv = lax.broadcasted_iota(jnp.int32, (GLA_KEY_WIDTH, GLA_WIDTH), 1) // GLA_DV
        head_match = hk == hv
        expand = jnp.where(head_match, 1.0, 0.0).astype(BF16)
        n_chunk = BLK // GLA_CHUNK
        t_local = row % GLA_CHUNK

        def chunk_rows(ref, s, n):
            return jnp.concatenate(
                [jnp.broadcast_to(ref[pl.ds(c * GLA_CHUNK + s, 1), :], (n, ref.shape[1]))
                 for c in range(n_chunk)], axis=0)

        def upper_half(t):
            return t.reshape(n_chunk, 2, SUBLANES, t.shape[1])[:, 1].reshape(BLK // 2, t.shape[1])

        o_lo = jnp.zeros((BLK, GLA_WIDTH), F32)
        for s in range(SUBLANES):
            term = q * chunk_rows(gk_ref, s, GLA_CHUNK) * jnp.exp2(g_cum - chunk_rows(gcum_ref, s, GLA_CHUNK))
            term = jnp.where(t_local >= s, term, 0.0)
            o_lo = o_lo + _dot(term.astype(BF16), expand) * chunk_rows(gv_ref, s, GLA_CHUNK)
        q_up, g_up, t_up = upper_half(q), upper_half(g_cum), upper_half(t_local)
        o_up = jnp.zeros((BLK // 2, GLA_WIDTH), F32)
        for s in range(SUBLANES, GLA_CHUNK):
            term = q_up * chunk_rows(gk_ref, s, SUBLANES) * jnp.exp2(g_up - chunk_rows(gcum_ref, s, SUBLANES))
            term = jnp.where(t_up >= s, term, 0.0)
            o_up = o_up + _dot(term.astype(BF16), expand) * chunk_rows(gv_ref, s, SUBLANES)
        o_up = jnp.concatenate(
            [jnp.zeros((n_chunk, 1, SUBLANES, GLA_WIDTH), F32),
             o_up.reshape(n_chunk, 1, SUBLANES, GLA_WIDTH)], axis=1).reshape(BLK, GLA_WIDTH)
        o_gla = o_lo + o_up

        k_dec_t = k_dec.T
        lhs = jnp.concatenate(
            [jnp.where(col // GLA_CHUNK == jj, k_dec_t, 0.0) for jj in range(n_chunk)], axis=0)
        upd = _dot(lhs.astype(BF16), gv.astype(BF16))
        a_t = a_chunk.T
        state = state_ref[...]
        outs = []
        for jj in range(n_chunk):
            qj = q_dec[jj * GLA_CHUNK:(jj + 1) * GLA_CHUNK].astype(BF16)
            outs.append(_dot(qj, state.astype(BF16)))
            a_j = a_t[:, jj * GLA_CHUNK:jj * GLA_CHUNK + 1]
            state = a_j * state + jnp.where(head_match, upd[jj * BLK:(jj + 1) * BLK], 0.0)
        state_ref[...] = state
        o_gla = o_gla + jnp.concatenate(outs, axis=0)
        seg_r = lax.broadcasted_iota(jnp.int32, (GLA_WIDTH, GLA_WIDTH), 0) // GLA_DV
        seg_c = lax.broadcasted_iota(jnp.int32, (GLA_WIDTH, GLA_WIDTH), 1) // GLA_DV
        head_ones = jnp.where(seg_r == seg_c, 1.0, 0.0).astype(BF16)
        ms = _split_dot_rhs(o_gla * o_gla, head_ones) * (1.0 / GLA_DV)
        yb = o_gla * lax.rsqrt(ms + NORM_EPS) * gng_ref[...]
        yb = yb * _silu(seg(C_GLA_Z, GLA_WIDTH))
        y_ref[rows, Y_B:Y_B + GLA_WIDTH] = yb.astype(BF16)

        cosb = cos_ref[rows, :]
        sinb = sin_ref[rows, :]

        def rope(t):
            return t * cosb + pltpu.roll(t, LANES // 2, 1) * sinb

        lane = lax.broadcasted_iota(jnp.int32, (BLK, LANES), 1)
        lane2 = lax.broadcasted_iota(jnp.int32, (2 * BLK, LANES), 1)
        k_cur = [rope(seg(C_SWA_K + LANES * g, LANES)).astype(BF16) for g in range(SWA_KV_HEADS)]
        v_cur = seg(C_SWA_V, SWA_KV_WIDTH).astype(BF16)
        k_all = [jnp.concatenate([kprev_ref[g], k_cur[g]], axis=0)
                 for g in range(SWA_KV_HEADS)]
        v_all = jnp.concatenate([vprev_ref[...], v_cur], axis=0)
        one = jnp.ones((), BF16)
        v_aug = [jnp.where(lane2 < HEAD_DIM, v_all, one), jnp.where(lane2 < HEAD_DIM, one, v_all)]
        tq = lax.broadcasted_iota(jnp.int32, (BLK, 2 * BLK), 0)
        kc = lax.broadcasted_iota(jnp.int32, (BLK, 2 * BLK), 1)
        first_block = (step * n_sub + i) == 0
        prev_lo = jnp.where(first_block, BLK, 0)
        in_window = ((kc < BLK) & (kc > tq + prev_lo)) | ((kc >= BLK) & (kc - BLK <= tq))
        bias = jnp.where(in_window, 0.0, -jnp.inf)
        lane_parity = (lane // (HEAD_DIM // 2)) % 2
        pv = [None] * SWA_Q_HEADS
        sink_w = [None] * SWA_Q_HEADS
        for p in range(SWA_Q_HEADS // 2):
            qp = rope(seg(C_SWA_Q + LANES * p, LANES))
            qp = qp * (HEAD_DIM ** -0.5 * LOG2E)
            g = p // (SWA_GROUP // 2)
            for par in range(2):
                hh = 2 * p + par
                qm = jnp.where(lane_parity == par, qp, 0.0).astype(BF16)
                sc = _dot_nt(qm, k_all[g]) + bias
                sink = sinks_ref[hh] * LOG2E
                m = jnp.maximum(jnp.max(sc, axis=-1, keepdims=True), sink)
                pv[hh] = _dot(jnp.exp2(sc - m).astype(BF16), v_aug[g])
                sink_w[hh] = jnp.exp2(sink - m)
        for c in range(SWA_GROUP):
            lo_head = lane < HEAD_DIM
            num = jnp.where(lo_head, pv[c], pv[SWA_GROUP + c])
            den = pltpu.roll(jnp.where(lo_head, pv[SWA_GROUP + c], pv[c]), LANES // 2, 1)
            den = den + jnp.where(lo_head, sink_w[c], sink_w[SWA_GROUP + c])
            yc = num / den * _silu(seg(C_SWA_Z + LANES * c, LANES))
            y_ref[rows, Y_C + LANES * c:Y_C + LANES * (c + 1)] = yc.astype(BF16)
        for g in range(SWA_KV_HEADS):
            kprev_ref[g] = k_cur[g]
        vprev_ref[...] = v_cur
        return carry

    lax.fori_loop(0, n_sub, sub_block, 0)

    y2 = _dot(y_ref[...], w_out_ref[...])
    y2 = y2 * lax.rsqrt(jnp.mean(y2 * y2, axis=-1, keepdims=True) + NORM_EPS) * post_g_ref[...]
    o_ref[...] = x_ref[...] + y2


def _rope_tables(positions):
    n = positions.size
    half = HEAD_DIM // 2
    inv_freq = jnp.power(ROPE_THETA, -jnp.arange(half, dtype=F32) * 2.0 / HEAD_DIM)
    invf = jnp.tile(inv_freq, LANES // half).reshape(1, LANES)
    pos = jnp.broadcast_to(positions.reshape(n, 1).astype(F32), (n, LANES))
    rows = 2048
    assert n % rows == 0
    spec = pl.BlockSpec((rows, LANES), lambda i: (i, 0))
    return pl.pallas_call(
        _rope_table_kernel,
        grid=(n // rows,),
        in_specs=[spec, pl.BlockSpec((1, LANES), lambda i: (0, 0))],
        out_specs=[spec, spec],
        out_shape=[jax.ShapeDtypeStruct((n, LANES), F32)] * 2,
        name="rope_tables",
    )(pos, invf)


def _layer_call(batch, seq):
    tb = TOKEN_BLOCK
    n_sub = tb // BLK
    steps = seq // tb
    tok = lambda b, j: (b * steps + j, 0)
    const2 = lambda b, j: (0, 0)
    const3 = lambda b, j: (0, 0, 0)
    in_specs = [
        pl.BlockSpec(memory_space=pltpu.SMEM),
        pl.BlockSpec((tb, D_MODEL), tok),
        pl.BlockSpec((tb, LANES), tok),
        pl.BlockSpec((tb, LANES), tok),
        pl.BlockSpec((1, D_MODEL), const2),
        pl.BlockSpec((D_MODEL, D_IN_PAD), const2),
        pl.BlockSpec((1, GM_WIDTH), const2),
        pl.BlockSpec((1, GM_WIDTH), const2),
        pl.BlockSpec((GM_HEADS, BLK, BLK), const3),
        pl.BlockSpec((BLK, GM_WIDTH), const2),
        pl.BlockSpec((LANES, GLA_KEY_WIDTH), const2),
        pl.BlockSpec((1, GLA_KEY_WIDTH), const2),
        pl.BlockSpec((1, GLA_WIDTH), const2),
        pl.BlockSpec((D_MIX, D_MODEL), const2),
        pl.BlockSpec((1, D_MODEL), const2),
    ]
    return pl.pallas_call(
        functools.partial(_layer_kernel, n_sub=n_sub),
        grid=(batch, steps),
        in_specs=in_specs,
        out_specs=pl.BlockSpec((tb, D_MODEL), tok),
        out_shape=jax.ShapeDtypeStruct((batch * seq, D_MODEL), F32),
        scratch_shapes=[
            pltpu.VMEM((tb, D_IN_PAD), F32),
            pltpu.VMEM((tb, D_MIX), BF16),
            pltpu.VMEM((GLA_KEY_WIDTH, GLA_WIDTH), F32),
            pltpu.VMEM((SWA_KV_HEADS, BLK, LANES), BF16),
            pltpu.VMEM((BLK, SWA_KV_WIDTH), BF16),
            pltpu.VMEM((BLK, GLA_KEY_WIDTH), F32),
            pltpu.VMEM((BLK, GLA_KEY_WIDTH), F32),
            pltpu.VMEM((BLK, GLA_WIDTH), F32),
        ],
        compiler_params=pltpu.CompilerParams(
            dimension_semantics=("arbitrary", "arbitrary"),
            vmem_limit_bytes=VMEM_LIMIT_BYTES),
        name="hybrid_layer",
    )


def kernel(x, positions, pre_g, w_in, gm_ln_g, gm_ln_b, gm_ws, gm_bs, gla_wg2, gla_bg, gla_norm_g,
           swa_sinks, w_out, post_g):
    batch, seq, d_model = x.shape
    assert d_model == D_MODEL and seq % TOKEN_BLOCK == 0
    assert w_in.shape == (DEPTH, D_MODEL, D_IN) and w_out.shape == (DEPTH, D_MIX, D_MODEL)

    w_in_p = _take_static(w_in, _in_proj_columns(), axis=2).astype(BF16)
    out_rows = np.concatenate([np.arange(Y_C), Y_C + _swa_out_order()])
    w_out_p = _take_static(w_out, out_rows, axis=1).astype(BF16)
    bs_p = jnp.repeat(jnp.swapaxes(gm_bs, 1, 2), HEAD_DIM, axis=2)
    wg2_p = jnp.pad(gla_wg2, ((0, 0), (0, LANES - GLA_GATE_RANK), (0, 0))).astype(BF16)
    gng_p = jnp.tile(gla_norm_g, (1, GLA_HEADS))

    cos_t, sin_t = _rope_tables(positions)
    layer = _layer_call(batch, seq)
    h = x.reshape(batch * seq, D_MODEL)
    for l in range(DEPTH):
        h = layer(swa_sinks[l], h, cos_t, sin_t, pre_g[l][None], w_in_p[l], gm_ln_g[l][None],
                  gm_ln_b[l][None], gm_ws[l], bs_p[l], wg2_p[l], gla_bg[l][None], gng_p[l][None],
                  w_out_p[l], post_g[l][None])
    return h.reshape(batch, seq, D_MODEL)
```

```python
import functools
import math

import numpy as np
import jax
import jax.numpy as jnp
from jax import lax
from jax.experimental import pallas as pl
from jax.experimental.pallas import tpu as pltpu

F32 = jnp.float32
BF16 = jnp.bfloat16

D_MODEL = 1024
DEPTH = 4
HEAD_DIM = 64
NORM_EPS = 1e-6
GM_HEADS = 4
GM_WIDTH = 256
GLA_HEADS = 4
GLA_DV = 64
GLA_DK = 32
GLA_WIDTH = 256
GLA_KEY_WIDTH = 128
GLA_GATE_RANK = 16
GLA_TAU = 16.0
GLA_CHUNK = 16
SWA_Q_HEADS = 8
SWA_KV_HEADS = 2
SWA_GROUP = SWA_Q_HEADS // SWA_KV_HEADS
SWA_WIDTH = 512
SWA_KV_WIDTH = 128
ROPE_THETA = 10000.0
D_MIX = 1024
D_IN = 2832
LOG2E = math.log2(math.e)

BLK = 128
LANES = 128
SUBLANES = 8
TOKEN_BLOCK = 512
VMEM_LIMIT_BYTES = 56 * 1024 * 1024

C_GM_U, C_GM_V, C_GM_Z = 0, 256, 512
C_GLA_Q, C_GLA_K, C_GLA_V, C_GLA_Z = 768, 896, 1024, 1280
C_SWA_Q, C_SWA_K, C_SWA_V, C_SWA_Z = 1536, 2048, 2304, 2432
C_GLA_GLR = 2944
D_IN_PAD = 3072
Y_A, Y_B, Y_C = 0, 256, 512


def _swa_out_order():
    r = np.arange(HEAD_DIM)
    return np.concatenate(
        [np.concatenate([64 * c + r, 64 * (SWA_GROUP + c) + r]) for c in range(SWA_GROUP)])


def _in_proj_columns():
    o = np.cumsum((0, 256, 256, 256, 128, 128, 256, 16, 256, 512, 128, 128, 512))
    (o_u, o_v, o_z, o_q, o_k, o_gv, o_glr, o_gz, o_sq, o_sk, o_sv, o_sz, _) = o
    half = HEAD_DIM // 2
    r = np.arange
    cols = [o_u + r(256), o_v + r(256), o_z + r(256),
            o_q + r(128), o_k + r(128), o_gv + r(256), o_gz + r(256)]
    for p in range(SWA_Q_HEADS // 2):
        a, b = o_sq + 64 * (2 * p), o_sq + 64 * (2 * p + 1)
        cols += [a + r(half), b + r(half), a + half + r(half), b + half + r(half)]
    for g in range(SWA_KV_HEADS):
        a = o_sk + 64 * g
        cols += [a + r(half), a + r(half), a + half + r(half), a + half + r(half)]
    cols += [o_sv + r(128)]
    cols += [_swa_out_order() + o_sz]
    cols += [o_glr + r(GLA_GATE_RANK), np.full(LANES - GLA_GATE_RANK, -1)]
    cols = np.concatenate(cols)
    assert cols.shape == (D_IN_PAD,)
    return cols


def _take_static(w, idx, axis):
    idx = np.asarray(idx)
    pieces, start = [], 0
    for end in range(1, len(idx) + 1):
        run_ends = end == len(idx) or (idx[end] != idx[end - 1] + 1) or idx[end] < 0 or idx[end - 1] < 0
        if not run_ends:
            continue
        n = end - start
        if idx[start] < 0:
            shape = list(w.shape)
            shape[axis] = n
            pieces.append(jnp.zeros(shape, w.dtype))
        else:
            pieces.append(lax.slice_in_dim(w, int(idx[start]), int(idx[start]) + n, axis=axis))
        start = end
    return jnp.concatenate(pieces, axis=axis)


def _silu(z):
    hz = 0.5 * z
    return hz + hz * jnp.tanh(hz)


def _dot(a, b):
    return jnp.dot(a, b, preferred_element_type=F32)


def _dot_nt(a, b):
    return lax.dot_general(a, b, (((1,), (1,)), ((), ())), preferred_element_type=F32)


def _split_dot(m, x):
    hi = x.astype(BF16)
    lo = (x - hi.astype(F32)).astype(BF16)
    return _dot(m, hi) + _dot(m, lo)


def _split_dot_rhs(x, m):
    hi = x.astype(BF16)
    lo = (x - hi.astype(F32)).astype(BF16)
    return _dot(hi, m) + _dot(lo, m)


def _rope_table_kernel(pos_ref, invf_ref, cos_ref, sin_ref):
    ang = pos_ref[...] * invf_ref[...]
    lane = lax.broadcasted_iota(jnp.int32, ang.shape, 1)
    cos_ref[...] = jnp.cos(ang)
    sin_ref[...] = jnp.where(lane < LANES // 2, -jnp.sin(ang), jnp.sin(ang))


def _layer_kernel(sinks_ref, x_ref, cos_ref, sin_ref, pre_g_ref, w_in_ref, lng_ref, lnb_ref,
                  ws_ref, bs_ref, wg2_ref, bg_ref, gng_ref, w_out_ref, post_g_ref,
                  o_ref, proj_ref, y_ref, state_ref, kprev_ref, vprev_ref,
                  gcum_ref, gk_ref, gv_ref, *, n_sub):
    step = pl.program_id(1)

    @pl.when(step == 0)
    def _():
        state_ref[...] = jnp.zeros_like(state_ref)
        kprev_ref[...] = jnp.zeros_like(kprev_ref)
        vprev_ref[...] = jnp.zeros_like(vprev_ref)

    x = x_ref[...]
    h = x * lax.rsqrt(jnp.mean(x * x, axis=-1, keepdims=True) + NORM_EPS) * pre_g_ref[...]
    proj_ref[...] = _dot(h.astype(BF16), w_in_ref[...])

    def sub_block(i, carry):
        rows = pl.ds(pl.multiple_of(i * BLK, BLK), BLK)
        row = lax.broadcasted_iota(jnp.int32, (BLK, BLK), 0)
        col = lax.broadcasted_iota(jnp.int32, (BLK, BLK), 1)
        lane = lax.broadcasted_iota(jnp.int32, (BLK, LANES), 1)
        n_chunk = BLK // GLA_CHUNK

        def seg(c0, width):
            return proj_ref[rows, c0:c0 + width]

        v = seg(C_GM_V, GM_WIDTH)
        mu = jnp.mean(v, axis=-1, keepdims=True)
        vc = v - mu
        var = jnp.mean(vc * vc, axis=-1, keepdims=True)

        glr = seg(C_GLA_GLR, LANES).astype(BF16)
        logit = _dot(glr, wg2_ref[...]) + bg_ref[...]

        cosb = cos_ref[rows, :]
        sinb = sin_ref[rows, :]

        def rope(t):
            return t * cosb + pltpu.roll(t, LANES // 2, 1) * sinb

        lane2 = lax.broadcasted_iota(jnp.int32, (2 * BLK, LANES), 1)
        k_cur = [rope(seg(C_SWA_K + LANES * g, LANES)).astype(BF16) for g in range(SWA_KV_HEADS)]
        v_cur = seg(C_SWA_V, SWA_KV_WIDTH).astype(BF16)
        k_all = [jnp.concatenate([kprev_ref[g], k_cur[g]], axis=0)
                 for g in range(SWA_KV_HEADS)]
        v_all = jnp.concatenate([vprev_ref[...], v_cur], axis=0)
        one = jnp.ones((), BF16)
        v_aug = [jnp.where(lane2 < HEAD_DIM, v_all, one), jnp.where(lane2 < HEAD_DIM, one, v_all)]
        tq = lax.broadcasted_iota(jnp.int32, (BLK, 2 * BLK), 0)
        kc = lax.broadcasted_iota(jnp.int32, (BLK, 2 * BLK), 1)
        first_block = (step * n_sub + i) == 0
        prev_lo = jnp.where(first_block, BLK, 0)
        in_window = ((kc < BLK) & (kc > tq + prev_lo)) | ((kc >= BLK) & (kc - BLK <= tq))
        bias = jnp.where(in_window, 0.0, -jnp.inf)

        vln = (vc * lax.rsqrt(var + NORM_EPS) * lng_ref[...] + lnb_ref[...]).astype(BF16)
        sv_h = [_dot(jnp.where(col <= row, ws_ref[hh], 0.0).astype(BF16), vln) for hh in range(GM_HEADS)]

        log_a = (jnp.minimum(logit, 0.0) - jnp.log1p(jnp.exp(-jnp.abs(logit)))) * (LOG2E / GLA_TAU)
        same_chunk = (row // GLA_CHUNK) == (col // GLA_CHUNK)
        tri16 = jnp.where(same_chunk & (col <= row), 1.0, 0.0).astype(BF16)
        ones16 = jnp.where(same_chunk, 1.0, 0.0).astype(BF16)
        g_cum = _split_dot(tri16, log_a)
        g_end = _split_dot(ones16, log_a)

        q_pairs = [rope(seg(C_SWA_Q + LANES * p, LANES)) * (HEAD_DIM ** -0.5 * LOG2E)
                   for p in range(SWA_Q_HEADS // 2)]
        lane_parity = (lane // (HEAD_DIM // 2)) % 2

        lane_head = lax.broadcasted_iota(jnp.int32, (BLK, GM_WIDTH), 1) // HEAD_DIM
        sv = sv_h[0]
        for hh in range(1, GM_HEADS):
            sv = jnp.where(lane_head == hh, sv_h[hh], sv)
        ya = seg(C_GM_U, GM_WIDTH) * (sv + bs_ref[...]) * _silu(seg(C_GM_Z, GM_WIDTH))
        y_ref[rows, Y_A:Y_A + GM_WIDTH] = ya.astype(BF16)

        q = seg(C_GLA_Q, GLA_KEY_WIDTH) * (GLA_DK ** -0.5)
        k = seg(C_GLA_K, GLA_KEY_WIDTH)
        gv = seg(C_GLA_V, GLA_WIDTH)
        q_dec = q * jnp.exp2(g_cum)
        k_dec = k * jnp.exp2(g_end - g_cum)
        a_t = jnp.exp2(g_end).T
        gcum_ref[...] = g_cum
        gk_ref[...] = k
        gv_ref[...] = gv
        hk = lax.broadcasted_iota(jnp.int32, (GLA_KEY_WIDTH, GLA_WIDTH), 0) // GLA_DK
        hv = lax.broadcasted_iota(jnp.int32, (GLA_KEY_WIDTH, GLA_WIDTH), 1) // GLA_DV
        head_match = hk == hv
        expand = jnp.where(head_match, 1.0, 0.0).astype(BF16)
        k_dec_t = k_dec.T
        lhs = jnp.concatenate(
            [jnp.where(col // GLA_CHUNK == jj, k_dec_t, 0.0) for jj in range(n_chunk)], axis=0)
        upd = _dot(lhs.astype(BF16), gv.astype(BF16))

        t_local = row % GLA_CHUNK

        def chunk_rows(ref, s, n):
            return jnp.concatenate(
                [jnp.broadcast_to(ref[pl.ds(c * GLA_CHUNK + s, 1), :], (n, ref.shape[1]))
                 for c in range(n_chunk)], axis=0)

        def upper_half(t):
            return t.reshape(n_chunk, 2, SUBLANES, t.shape[1])[:, 1].reshape(BLK // 2, t.shape[1])

        q_up, g_up, t_up = upper_half(q), upper_half(g_cum), upper_half(t_local)

        def intra_term(s):
            if s < SUBLANES:
                qq, gg, tt, n = q, g_cum, t_local, GLA_CHUNK
            else:
                qq, gg, tt, n = q_up, g_up, t_up, SUBLANES
            term = qq * chunk_rows(gk_ref, s, n) * jnp.exp2(gg - chunk_rows(gcum_ref, s, n))
            term = jnp.where(tt >= s, term, 0.0)
            return _dot(term.astype(BF16), expand) * chunk_rows(gv_ref, s, n)

        def attention_head(hh):
            p, par = divmod(hh, 2)
            g = hh // SWA_GROUP
            qm = jnp.where(lane_parity == par, q_pairs[p], 0.0).astype(BF16)
            sc = _dot_nt(qm, k_all[g]) + bias
            sink = sinks_ref[hh] * LOG2E
            m = jnp.maximum(jnp.max(sc, axis=-1, keepdims=True), sink)
            return _dot(jnp.exp2(sc - m).astype(BF16), v_aug[g]), jnp.exp2(sink - m)

        o_lo = jnp.zeros((BLK, GLA_WIDTH), F32)
        o_up = jnp.zeros((BLK // 2, GLA_WIDTH), F32)
        state = state_ref[...]
        outs, pv, sink_w = [], [None] * SWA_Q_HEADS, [None] * SWA_Q_HEADS
        for jj in range(n_chunk):
            o_lo = o_lo + intra_term(jj)
            o_up = o_up + intra_term(SUBLANES + jj)
            pv[jj], sink_w[jj] = attention_head(jj)
            qj = q_dec[jj * GLA_CHUNK:(jj + 1) * GLA_CHUNK].astype(BF16)
            outs.append(_dot(qj, state.astype(BF16)))
            a_j = a_t[:, jj * GLA_CHUNK:jj * GLA_CHUNK + 1]
            state = a_j * state + jnp.where(head_match, upd[jj * BLK:(jj + 1) * BLK], 0.0)
        state_ref[...] = state

        for c in range(SWA_GROUP):
            lo_head = lane < HEAD_DIM
            num = jnp.where(lo_head, pv[c], pv[SWA_GROUP + c])
            den = pltpu.roll(jnp.where(lo_head, pv[SWA_GROUP + c], pv[c]), LANES // 2, 1)
            den = den + jnp.where(lo_head, sink_w[c], sink_w[SWA_GROUP + c])
            yc = num / den * _silu(seg(C_SWA_Z + LANES * c, LANES))
            y_ref[rows, Y_C + LANES * c:Y_C + LANES * (c + 1)] = yc.astype(BF16)
        for g in range(SWA_KV_HEADS):
            kprev_ref[g] = k_cur[g]
        vprev_ref[...] = v_cur

        o_up = jnp.concatenate(
            [jnp.zeros((n_chunk, 1, SUBLANES, GLA_WIDTH), F32),
             o_up.reshape(n_chunk, 1, SUBLANES, GLA_WIDTH)], axis=1).reshape(BLK, GLA_WIDTH)
        o_gla = o_lo + o_up + jnp.concatenate(outs, axis=0)
        seg_r = lax.broadcasted_iota(jnp.int32, (GLA_WIDTH, GLA_WIDTH), 0) // GLA_DV
        seg_c = lax.broadcasted_iota(jnp.int32, (GLA_WIDTH, GLA_WIDTH), 1) // GLA_DV
        head_ones = jnp.where(seg_r == seg_c, 1.0, 0.0).astype(BF16)
        ms = _split_dot_rhs(o_gla * o_gla, head_ones) * (1.0 / GLA_DV)
        yb = o_gla * lax.rsqrt(ms + NORM_EPS) * gng_ref[...]
        yb = yb * _silu(seg(C_GLA_Z, GLA_WIDTH))
        y_ref[rows, Y_B:Y_B + GLA_WIDTH] = yb.astype(BF16)
        return carry

    lax.fori_loop(0, n_sub, sub_block, 0)

    y2 = _dot(y_ref[...], w_out_ref[...])
    y2 = y2 * lax.rsqrt(jnp.mean(y2 * y2, axis=-1, keepdims=True) + NORM_EPS) * post_g_ref[...]
    o_ref[...] = x_ref[...] + y2


def _rope_tables(positions):
    n = positions.size
    half = HEAD_DIM // 2
    inv_freq = jnp.power(ROPE_THETA, -jnp.arange(half, dtype=F32) * 2.0 / HEAD_DIM)
    invf = jnp.tile(inv_freq, LANES // half).reshape(1, LANES)
    pos = jnp.broadcast_to(positions.reshape(n, 1).astype(F32), (n, LANES))
    rows = 2048
    assert n % rows == 0
    spec = pl.BlockSpec((rows, LANES), lambda i: (i, 0))
    return pl.pallas_call(
        _rope_table_kernel,
        grid=(n // rows,),
        in_specs=[spec, pl.BlockSpec((1, LANES), lambda i: (0, 0))],
        out_specs=[spec, spec],
        out_shape=[jax.ShapeDtypeStruct((n, LANES), F32)] * 2,
        name="rope_tables",
    )(pos, invf)


def _layer_call(batch, seq):
    tb = TOKEN_BLOCK
    n_sub = tb // BLK
    steps = seq // tb
    tok = lambda b, j: (b * steps + j, 0)
    const2 = lambda b, j: (0, 0)
    const3 = lambda b, j: (0, 0, 0)
    in_specs = [
        pl.BlockSpec(memory_space=pltpu.SMEM),
        pl.BlockSpec((tb, D_MODEL), tok),
        pl.BlockSpec((tb, LANES), tok),
        pl.BlockSpec((tb, LANES), tok),
        pl.BlockSpec((1, D_MODEL), const2),
        pl.BlockSpec((D_MODEL, D_IN_PAD), const2),
        pl.BlockSpec((1, GM_WIDTH), const2),
        pl.BlockSpec((1, GM_WIDTH), const2),
        pl.BlockSpec((GM_HEADS, BLK, BLK), const3),
        pl.BlockSpec((BLK, GM_WIDTH), const2),
        pl.BlockSpec((LANES, GLA_KEY_WIDTH), const2),
        pl.BlockSpec((1, GLA_KEY_WIDTH), const2),
        pl.BlockSpec((1, GLA_WIDTH), const2),
        pl.BlockSpec((D_MIX, D_MODEL), const2),
        pl.BlockSpec((1, D_MODEL), const2),
    ]
    return pl.pallas_call(
        functools.partial(_layer_kernel, n_sub=n_sub),
        grid=(batch, steps),
        in_specs=in_specs,
        out_specs=pl.BlockSpec((tb, D_MODEL), tok),
        out_shape=jax.ShapeDtypeStruct((batch * seq, D_MODEL), F32),
        scratch_shapes=[
            pltpu.VMEM((tb, D_IN_PAD), F32),
            pltpu.VMEM((tb, D_MIX), BF16),
            pltpu.VMEM((GLA_KEY_WIDTH, GLA_WIDTH), F32),
            pltpu.VMEM((SWA_KV_HEADS, BLK, LANES), BF16),
            pltpu.VMEM((BLK, SWA_KV_WIDTH), BF16),
            pltpu.VMEM((BLK, GLA_KEY_WIDTH), F32),
            pltpu.VMEM((BLK, GLA_KEY_WIDTH), F32),
            pltpu.VMEM((BLK, GLA_WIDTH), F32),
        ],
        compiler_params=pltpu.CompilerParams(
            dimension_semantics=("arbitrary", "arbitrary"),
            vmem_limit_bytes=VMEM_LIMIT_BYTES),
        name="hybrid_layer",
    )


def kernel(x, positions, pre_g, w_in, gm_ln_g, gm_ln_b, gm_ws, gm_bs, gla_wg2, gla_bg, gla_norm_g,
           swa_sinks, w_out, post_g):
    batch, seq, d_model = x.shape
    assert d_model == D_MODEL and seq % TOKEN_BLOCK == 0
    assert w_in.shape == (DEPTH, D_MODEL, D_IN) and w_out.shape == (DEPTH, D_MIX, D_MODEL)

    w_in_p = _take_static(w_in, _in_proj_columns(), axis=2).astype(BF16)
    out_rows = np.concatenate([np.arange(Y_C), Y_C + _swa_out_order()])
    w_out_p = _take_static(w_out, out_rows, axis=1).astype(BF16)
    bs_p = jnp.repeat(jnp.swapaxes(gm_bs, 1, 2), HEAD_DIM, axis=2)
    wg2_p = jnp.pad(gla_wg2, ((0, 0), (0, LANES - GLA_GATE_RANK), (0, 0))).astype(BF16)
    gng_p = jnp.tile(gla_norm_g, (1, GLA_HEADS))

    cos_t, sin_t = _rope_tables(positions)
    layer = _layer_call(batch, seq)
    h = x.reshape(batch * seq, D_MODEL)
    for l in range(DEPTH):
        h = layer(swa_sinks[l], h, cos_t, sin_t, pre_g[l][None], w_in_p[l], gm_ln_g[l][None],
                  gm_ln_b[l][None], gm_ws[l], bs_p[l], wg2_p[l], gla_bg[l][None], gng_p[l][None],
                  w_out_p[l], post_g[l][None])
    return h.reshape(batch, seq, D_MODEL)
```

```python
import functools
import math

import numpy as np
import jax
import jax.numpy as jnp
from jax import lax
from jax.experimental import pallas as pl
from jax.experimental.pallas import tpu as pltpu

F32 = jnp.float32
BF16 = jnp.bfloat16

D_MODEL = 1024
DEPTH = 4
HEAD_DIM = 64
NORM_EPS = 1e-6
GM_HEADS = 4
GM_WIDTH = 256
GLA_HEADS = 4
GLA_DV = 64
GLA_DK = 32
GLA_WIDTH = 256
GLA_KEY_WIDTH = 128
GLA_GATE_RANK = 16
GLA_TAU = 16.0
GLA_CHUNK = 16
SWA_Q_HEADS = 8
SWA_KV_HEADS = 2
SWA_GROUP = SWA_Q_HEADS // SWA_KV_HEADS
SWA_WIDTH = 512
SWA_KV_WIDTH = 128
ROPE_THETA = 10000.0
D_MIX = 1024
D_IN = 2832
LOG2E = math.log2(math.e)

BLK = 128
LANES = 128
SUBLANES = 8
TOKEN_BLOCK = 1024
VMEM_LIMIT_BYTES = 56 * 1024 * 1024

C_GM_U, C_GM_V, C_GM_Z = 0, 256, 512
C_GLA_Q, C_GLA_K, C_GLA_V, C_GLA_Z = 768, 896, 1024, 1280
C_SWA_Q, C_SWA_K, C_SWA_V, C_SWA_Z = 1536, 2048, 2304, 2432
C_GLA_GLR = 2944
D_IN_PAD = 3072
Y_A, Y_B, Y_C = 0, 256, 512


def _swa_out_order():
    r = np.arange(HEAD_DIM)
    return np.concatenate(
        [np.concatenate([64 * c + r, 64 * (SWA_GROUP + c) + r]) for c in range(SWA_GROUP)])


def _in_proj_columns():
    o = np.cumsum((0, 256, 256, 256, 128, 128, 256, 16, 256, 512, 128, 128, 512))
    (o_u, o_v, o_z, o_q, o_k, o_gv, o_glr, o_gz, o_sq, o_sk, o_sv, o_sz, _) = o
    half = HEAD_DIM // 2
    r = np.arange
    cols = [o_u + r(256), o_v + r(256), o_z + r(256),
            o_q + r(128), o_k + r(128), o_gv + r(256), o_gz + r(256)]
    for p in range(SWA_Q_HEADS // 2):
        a, b = o_sq + 64 * (2 * p), o_sq + 64 * (2 * p + 1)
        cols += [a + r(half), b + r(half), a + half + r(half), b + half + r(half)]
    for g in range(SWA_KV_HEADS):
        a = o_sk + 64 * g
        cols += [a + r(half), a + r(half), a + half + r(half), a + half + r(half)]
    cols += [o_sv + r(128)]
    cols += [_swa_out_order() + o_sz]
    cols += [o_glr + r(GLA_GATE_RANK), np.full(LANES - GLA_GATE_RANK, -1)]
    cols = np.concatenate(cols)
    assert cols.shape == (D_IN_PAD,)
    return cols


def _runs(idx):
    idx = np.asarray(idx)
    runs, start = [], 0
    for end in range(1, len(idx) + 1):
        if end < len(idx) and idx[end] == idx[end - 1] + 1 and idx[end - 1] >= 0:
            continue
        if end < len(idx) and idx[end] < 0 and idx[end - 1] < 0:
            continue
        runs.append((start, int(idx[start]), end - start))
        start = end
    return runs


def _relayout_kernel(w_ref, o_ref, *, runs, axis):
    for dst, src, n in runs:
        d = (slice(None),) * axis + (slice(dst, dst + n),)
        if src < 0:
            o_ref[d] = jnp.zeros(o_ref[d].shape, o_ref.dtype)
        else:
            o_ref[d] = w_ref[(slice(None),) * axis + (slice(src, src + n),)].astype(o_ref.dtype)


def _relayout(w, idx, axis, rows_per_step):
    depth, r, c = w.shape
    out_shape = (depth, len(idx), c) if axis == 1 else (depth, r, len(idx))
    in_block = (None, r, c) if axis == 1 else (None, rows_per_step, c)
    out_block = (None,) + out_shape[1:] if axis == 1 else (None, rows_per_step, len(idx))
    steps = 1 if axis == 1 else r // rows_per_step
    return pl.pallas_call(
        functools.partial(_relayout_kernel, runs=_runs(idx), axis=axis - 1),
        grid=(depth, steps),
        in_specs=[pl.BlockSpec(in_block, lambda l, i: (l, i, 0))],
        out_specs=pl.BlockSpec(out_block, lambda l, i: (l, i, 0)),
        out_shape=jax.ShapeDtypeStruct(out_shape, BF16),
        compiler_params=pltpu.CompilerParams(vmem_limit_bytes=VMEM_LIMIT_BYTES),
        name="weight_relayout",
    )(w)


def _silu(z):
    hz = 0.5 * z
    return hz + hz * jnp.tanh(hz)


def _dot(a, b):
    return jnp.dot(a, b, preferred_element_type=F32)


def _dot_nt(a, b):
    return lax.dot_general(a, b, (((1,), (1,)), ((), ())), preferred_element_type=F32)


def _split_dot(m, x):
    hi = x.astype(BF16)
    lo = (x - hi.astype(F32)).astype(BF16)
    return _dot(m, hi) + _dot(m, lo)


def _split_dot_rhs(x, m):
    hi = x.astype(BF16)
    lo = (x - hi.astype(F32)).astype(BF16)
    return _dot(hi, m) + _dot(lo, m)


def _rope_table_kernel(pos_ref, invf_ref, cos_ref, sin_ref):
    ang = pos_ref[...] * invf_ref[...]
    cos_ref[...] = jnp.cos(ang)
    sin_ref[...] = jnp.sin(ang)


def _layer_kernel(layer_ref, sinks_ref, x_ref, cos_ref, sin_ref, pre_g_ref, w_in_ref, lng_ref, lnb_ref,
                  ws_ref, bs_ref, wg2_ref, bg_ref, gng_ref, w_out_ref, post_g_ref,
                  o_ref, proj_ref, y_ref, state_ref, kprev_ref, vprev_ref,
                  gcum_ref, gk_ref, gv_ref, *, n_sub):
    step = pl.program_id(1)

    @pl.when(step == 0)
    def _():
        state_ref[...] = jnp.zeros_like(state_ref)
        kprev_ref[...] = jnp.zeros_like(kprev_ref)
        vprev_ref[...] = jnp.zeros_like(vprev_ref)

    x = x_ref[...]
    h = x * lax.rsqrt(jnp.mean(x * x, axis=-1, keepdims=True) + NORM_EPS) * pre_g_ref[...]
    proj_ref[...] = _dot(h.astype(BF16), w_in_ref[...])

    def sub_block(i, carry):
        rows = pl.ds(pl.multiple_of(i * BLK, BLK), BLK)
        row = lax.broadcasted_iota(jnp.int32, (BLK, BLK), 0)
        col = lax.broadcasted_iota(jnp.int32, (BLK, BLK), 1)
        lane = lax.broadcasted_iota(jnp.int32, (BLK, LANES), 1)
        n_chunk = BLK // GLA_CHUNK

        def seg(c0, width):
            return proj_ref[rows, c0:c0 + width]

        v = seg(C_GM_V, GM_WIDTH)
        mu = jnp.mean(v, axis=-1, keepdims=True)
        vc = v - mu
        var = jnp.mean(vc * vc, axis=-1, keepdims=True)

        glr = seg(C_GLA_GLR, LANES).astype(BF16)
        logit = _dot(glr, wg2_ref[...]) + bg_ref[...]

        cosb = cos_ref[rows, :]
        sinb = sin_ref[rows, :]

        def rope(t):
            return t * cosb + pltpu.roll(t, LANES // 2, 1) * sinb

        lane2 = lax.broadcasted_iota(jnp.int32, (2 * BLK, LANES), 1)
        k_cur = [rope(seg(C_SWA_K + LANES * g, LANES)).astype(BF16) for g in range(SWA_KV_HEADS)]
        v_cur = seg(C_SWA_V, SWA_KV_WIDTH).astype(BF16)
        k_all = [jnp.concatenate([kprev_ref[g], k_cur[g]], axis=0)
                 for g in range(SWA_KV_HEADS)]
        v_all = jnp.concatenate([vprev_ref[...], v_cur], axis=0)
        one = jnp.ones((), BF16)
        v_aug = [jnp.where(lane2 < HEAD_DIM, v_all, one), jnp.where(lane2 < HEAD_DIM, one, v_all)]
        tq = lax.broadcasted_iota(jnp.int32, (BLK, 2 * BLK), 0)
        kc = lax.broadcasted_iota(jnp.int32, (BLK, 2 * BLK), 1)
        first_block = (step * n_sub + i) == 0
        prev_lo = jnp.where(first_block, BLK, 0)
        in_window = ((kc < BLK) & (kc > tq + prev_lo)) | ((kc >= BLK) & (kc - BLK <= tq))
        bias = jnp.where(in_window, 0.0, -jnp.inf)

        vln = (vc * lax.rsqrt(var + NORM_EPS) * lng_ref[...] + lnb_ref[...]).astype(BF16)
        sv_h = [_dot(jnp.where(col <= row, ws_ref[hh], 0.0).astype(BF16), vln) for hh in range(GM_HEADS)]

        log_a = (jnp.minimum(logit, 0.0) - jnp.log1p(jnp.exp(-jnp.abs(logit)))) * (LOG2E / GLA_TAU)
        same_chunk = (row // GLA_CHUNK) == (col // GLA_CHUNK)
        tri16 = jnp.where(same_chunk & (col <= row), 1.0, 0.0).astype(BF16)
        ones16 = jnp.where(same_chunk, 1.0, 0.0).astype(BF16)
        g_cum = _split_dot(tri16, log_a)
        g_end = _split_dot(ones16, log_a)

        q_pairs = [rope(seg(C_SWA_Q + LANES * p, LANES)) * (HEAD_DIM ** -0.5 * LOG2E)
                   for p in range(SWA_Q_HEADS // 2)]
        lane_parity = (lane // (HEAD_DIM // 2)) % 2

        lane_head = lax.broadcasted_iota(jnp.int32, (BLK, GM_WIDTH), 1) // HEAD_DIM
        sv = sv_h[0]
        for hh in range(1, GM_HEADS):
            sv = jnp.where(lane_head == hh, sv_h[hh], sv)
        ya = seg(C_GM_U, GM_WIDTH) * (sv + bs_ref[...]) * _silu(seg(C_GM_Z, GM_WIDTH))
        y_ref[rows, Y_A:Y_A + GM_WIDTH] = ya.astype(BF16)

        q = seg(C_GLA_Q, GLA_KEY_WIDTH) * (GLA_DK ** -0.5)
        k = seg(C_GLA_K, GLA_KEY_WIDTH)
        gv = seg(C_GLA_V, GLA_WIDTH)
        gcum_ref[...] = g_cum
        gk_ref[...] = k
        gv_ref[...] = gv
        hk = lax.broadcasted_iota(jnp.int32, (GLA_KEY_WIDTH, GLA_WIDTH), 0) // GLA_DK
        hv = lax.broadcasted_iota(jnp.int32, (GLA_KEY_WIDTH, GLA_WIDTH), 1) // GLA_DV
        head_match = hk == hv
        expand = jnp.where(head_match, 1.0, 0.0).astype(BF16)

        def state_updates():
            k_dec = k * jnp.exp2(g_end - g_cum)
            k_dec_t = k_dec.T
            lhs = jnp.concatenate(
                [jnp.where(col // GLA_CHUNK == jj, k_dec_t, 0.0) for jj in range(n_chunk)], axis=0)
            return _dot(lhs.astype(BF16), gv.astype(BF16))

        upd = state_updates()
        q_dec = q * jnp.exp2(g_cum)
        a_t = jnp.exp2(g_end).T

        t_local = row % GLA_CHUNK

        def chunk_rows(ref, s, n):
            return jnp.concatenate(
                [jnp.broadcast_to(ref[pl.ds(c * GLA_CHUNK + s, 1), :], (n, ref.shape[1]))
                 for c in range(n_chunk)], axis=0)

        def upper_half(t):
            return t.reshape(n_chunk, 2, SUBLANES, t.shape[1])[:, 1].reshape(BLK // 2, t.shape[1])

        q_up, g_up, t_up = upper_half(q), upper_half(g_cum), upper_half(t_local)

        def intra_term(s):
            if s < SUBLANES:
                qq, gg, tt, n = q, g_cum, t_local, GLA_CHUNK
            else:
                qq, gg, tt, n = q_up, g_up, t_up, SUBLANES
            term = qq * chunk_rows(gk_ref, s, n) * jnp.exp2(gg - chunk_rows(gcum_ref, s, n))
            term = jnp.where(tt >= s, term, 0.0)
            return _dot(term.astype(BF16), expand) * chunk_rows(gv_ref, s, n)

        def attention_head(hh):
            p, par = divmod(hh, 2)
            g = hh // SWA_GROUP
            qm = jnp.where(lane_parity == par, q_pairs[p], 0.0).astype(BF16)
            sc = _dot_nt(qm, k_all[g]) + bias
            sink = sinks_ref[layer_ref[0], hh] * LOG2E
            m = jnp.maximum(jnp.max(sc, axis=-1, keepdims=True), sink)
            return _dot(jnp.exp2(sc - m).astype(BF16), v_aug[g]), jnp.exp2(sink - m)

        o_lo = jnp.zeros((BLK, GLA_WIDTH), F32)
        o_up = jnp.zeros((BLK // 2, GLA_WIDTH), F32)
        state = state_ref[...]
        outs, pv, sink_w = [], [None] * SWA_Q_HEADS, [None] * SWA_Q_HEADS
        for jj in range(n_chunk):
            o_lo = o_lo + intra_term(jj)
            o_up = o_up + intra_term(SUBLANES + jj)
            pv[jj], sink_w[jj] = attention_head(jj)
            qj = q_dec[jj * GLA_CHUNK:(jj + 1) * GLA_CHUNK].astype(BF16)
            outs.append(_dot(qj, state.astype(BF16)))
            a_j = a_t[:, jj * GLA_CHUNK:jj * GLA_CHUNK + 1]
            state = a_j * state + jnp.where(head_match, upd[jj * BLK:(jj + 1) * BLK], 0.0)
        state_ref[...] = state

        o_up = jnp.concatenate(
            [jnp.zeros((n_chunk, 1, SUBLANES, GLA_WIDTH), F32),
             o_up.reshape(n_chunk, 1, SUBLANES, GLA_WIDTH)], axis=1).reshape(BLK, GLA_WIDTH)
        o_gla = o_lo + o_up + jnp.concatenate(outs, axis=0)
        seg_r = lax.broadcasted_iota(jnp.int32, (GLA_WIDTH, GLA_WIDTH), 0) // GLA_DV
        seg_c = lax.broadcasted_iota(jnp.int32, (GLA_WIDTH, GLA_WIDTH), 1) // GLA_DV
        head_ones = jnp.where(seg_r == seg_c, 1.0, 0.0).astype(BF16)
        ms = _split_dot_rhs(o_gla * o_gla, head_ones) * (1.0 / GLA_DV)

        for c in range(SWA_GROUP):
            lo_head = lane < HEAD_DIM
            num = jnp.where(lo_head, pv[c], pv[SWA_GROUP + c])
            den = pltpu.roll(jnp.where(lo_head, pv[SWA_GROUP + c], pv[c]), LANES // 2, 1)
            den = den + jnp.where(lo_head, sink_w[c], sink_w[SWA_GROUP + c])
            yc = num / den * _silu(seg(C_SWA_Z + LANES * c, LANES))
            y_ref[rows, Y_C + LANES * c:Y_C + LANES * (c + 1)] = yc.astype(BF16)
        for g in range(SWA_KV_HEADS):
            kprev_ref[g] = k_cur[g]
        vprev_ref[...] = v_cur

        yb = o_gla * lax.rsqrt(ms + NORM_EPS) * gng_ref[...]
        yb = yb * _silu(seg(C_GLA_Z, GLA_WIDTH))
        y_ref[rows, Y_B:Y_B + GLA_WIDTH] = yb.astype(BF16)
        return carry

    lax.fori_loop(0, n_sub, sub_block, 0)

    y2 = _dot(y_ref[...], w_out_ref[...])
    y2 = y2 * lax.rsqrt(jnp.mean(y2 * y2, axis=-1, keepdims=True) + NORM_EPS) * post_g_ref[...]
    o_ref[...] = x_ref[...] + y2


def _rope_tables(positions):
    n = positions.size
    half = HEAD_DIM // 2
    inv_freq = jnp.power(ROPE_THETA, -jnp.arange(half, dtype=F32) * 2.0 / HEAD_DIM)
    per_row = LANES // half
    invf = jnp.tile(inv_freq, per_row).reshape(1, LANES)
    pos = jnp.repeat(positions.reshape(n // per_row, per_row).astype(F32), half, axis=1)
    rows = 1024
    assert (n // per_row) % rows == 0
    spec = pl.BlockSpec((rows, LANES), lambda i: (i, 0))
    cos_d, sin_d = pl.pallas_call(
        _rope_table_kernel,
        grid=(n // per_row // rows,),
        in_specs=[spec, pl.BlockSpec((1, LANES), lambda i: (0, 0))],
        out_specs=[spec, spec],
        out_shape=[jax.ShapeDtypeStruct((n // per_row, LANES), F32)] * 2,
        name="rope_tables",
    )(pos, invf)
    cos_t, sin_t = cos_d.reshape(n, half), sin_d.reshape(n, half)
    return (jnp.concatenate([cos_t] * per_row, axis=1),
            jnp.concatenate([-sin_t, -sin_t, sin_t, sin_t], axis=1))


def _layer_call(batch, seq):
    tb = TOKEN_BLOCK
    n_sub = tb // BLK
    steps = seq // tb
    tok = lambda b, j, l: (b * steps + j, 0)
    layer3 = lambda b, j, l: (l[0], 0, 0)
    layer4 = lambda b, j, l: (l[0], 0, 0, 0)
    in_specs = [
        pl.BlockSpec(memory_space=pltpu.SMEM),
        pl.BlockSpec((tb, D_MODEL), tok),
        pl.BlockSpec((tb, LANES), tok),
        pl.BlockSpec((tb, LANES), tok),
        pl.BlockSpec((None, 1, D_MODEL), layer3),
        pl.BlockSpec((None, D_MODEL, D_IN_PAD), layer3),
        pl.BlockSpec((None, 1, GM_WIDTH), layer3),
        pl.BlockSpec((None, 1, GM_WIDTH), layer3),
        pl.BlockSpec((None, GM_HEADS, BLK, BLK), layer4),
        pl.BlockSpec((None, BLK, GM_WIDTH), layer3),
        pl.BlockSpec((None, LANES, GLA_KEY_WIDTH), layer3),
        pl.BlockSpec((None, 1, GLA_KEY_WIDTH), layer3),
        pl.BlockSpec((None, 1, GLA_WIDTH), layer3),
        pl.BlockSpec((None, D_MIX, D_MODEL), layer3),
        pl.BlockSpec((None, 1, D_MODEL), layer3),
    ]
    return pl.pallas_call(
        functools.partial(_layer_kernel, n_sub=n_sub),
        grid_spec=pltpu.PrefetchScalarGridSpec(
            num_scalar_prefetch=1,
            grid=(batch, steps),
            in_specs=in_specs,
            out_specs=pl.BlockSpec((tb, D_MODEL), tok),
            scratch_shapes=[
                pltpu.VMEM((tb, D_IN_PAD), F32),
                pltpu.VMEM((tb, D_MIX), BF16),
                pltpu.VMEM((GLA_KEY_WIDTH, GLA_WIDTH), F32),
                pltpu.VMEM((SWA_KV_HEADS, BLK, LANES), BF16),
                pltpu.VMEM((BLK, SWA_KV_WIDTH), BF16),
                pltpu.VMEM((BLK, GLA_KEY_WIDTH), F32),
                pltpu.VMEM((BLK, GLA_KEY_WIDTH), F32),
                pltpu.VMEM((BLK, GLA_WIDTH), F32),
            ]),
        out_shape=jax.ShapeDtypeStruct((batch * seq, D_MODEL), F32),
        compiler_params=pltpu.CompilerParams(
            dimension_semantics=("arbitrary", "arbitrary"),
            vmem_limit_bytes=VMEM_LIMIT_BYTES),
        name="hybrid_layer",
    )


def kernel(x, positions, pre_g, w_in, gm_ln_g, gm_ln_b, gm_ws, gm_bs, gla_wg2, gla_bg, gla_norm_g,
           swa_sinks, w_out, post_g):
    batch, seq, d_model = x.shape
    assert d_model == D_MODEL and seq % TOKEN_BLOCK == 0
    assert w_in.shape == (DEPTH, D_MODEL, D_IN) and w_out.shape == (DEPTH, D_MIX, D_MODEL)

    row = lambda p: p[:, None, :]
    w_in_p = _relayout(w_in, _in_proj_columns(), axis=2, rows_per_step=256)
    out_rows = np.concatenate([np.arange(Y_C), Y_C + _swa_out_order()])
    w_out_p = _relayout(w_out, out_rows, axis=1, rows_per_step=D_MIX)
    bs_p = jnp.repeat(jnp.swapaxes(gm_bs, 1, 2), HEAD_DIM, axis=2)
    wg2_p = jnp.pad(gla_wg2, ((0, 0), (0, LANES - GLA_GATE_RANK), (0, 0))).astype(BF16)
    gng_p = jnp.tile(gla_norm_g, (1, GLA_HEADS))

    cos_t, sin_t = _rope_tables(positions)
    layer = _layer_call(batch, seq)
    h = x.reshape(batch * seq, D_MODEL)
    for l in range(DEPTH):
        h = layer(jnp.full((1,), l, jnp.int32), swa_sinks, h, cos_t, sin_t, row(pre_g), w_in_p,
                  row(gm_ln_g), row(gm_ln_b), gm_ws, bs_p, wg2_p, row(gla_bg), row(gng_p),
                  w_out_p, row(post_g))
    return h.reshape(batch, seq, D_MODEL)
```

```python
import functools
import math

import numpy as np
import jax
import jax.numpy as jnp
from jax import lax
from jax.experimental import pallas as pl
from jax.experimental.pallas import tpu as pltpu

F32 = jnp.float32
BF16 = jnp.bfloat16

D_MODEL = 1024
DEPTH = 4
HEAD_DIM = 64
NORM_EPS = 1e-6
GM_HEADS = 4
GM_WIDTH = 256
GLA_HEADS = 4
GLA_DV = 64
GLA_DK = 32
GLA_WIDTH = 256
GLA_KEY_WIDTH = 128
GLA_GATE_RANK = 16
GLA_TAU = 16.0
GLA_CHUNK = 16
SWA_Q_HEADS = 8
SWA_KV_HEADS = 2
SWA_GROUP = SWA_Q_HEADS // SWA_KV_HEADS
SWA_WIDTH = 512
SWA_KV_WIDTH = 128
ROPE_THETA = 10000.0
D_MIX = 1024
D_IN = 2832
LOG2E = math.log2(math.e)

BLK = 128
LANES = 128
SUBLANES = 8
TOKEN_BLOCK = 1024
VMEM_LIMIT_BYTES = 56 * 1024 * 1024

C_GM_U, C_GM_V, C_GM_Z = 0, 256, 512
C_GLA_Q, C_GLA_K, C_GLA_V, C_GLA_Z = 768, 896, 1024, 1280
C_SWA_Q, C_SWA_K, C_SWA_V, C_SWA_Z = 1536, 2048, 2304, 2432
C_GLA_GLR = 2944
D_IN_PAD = 3072
Y_A, Y_B, Y_C = 0, 256, 512


def _swa_out_order():
    r = np.arange(HEAD_DIM)
    return np.concatenate(
        [np.concatenate([64 * c + r, 64 * (SWA_GROUP + c) + r]) for c in range(SWA_GROUP)])


def _in_proj_columns():
    o = np.cumsum((0, 256, 256, 256, 128, 128, 256, 16, 256, 512, 128, 128, 512))
    (o_u, o_v, o_z, o_q, o_k, o_gv, o_glr, o_gz, o_sq, o_sk, o_sv, o_sz, _) = o
    half = HEAD_DIM // 2
    r = np.arange
    cols = [o_u + r(256), o_v + r(256), o_z + r(256),
            o_q + r(128), o_k + r(128), o_gv + r(256), o_gz + r(256)]
    for p in range(SWA_Q_HEADS // 2):
        a, b = o_sq + 64 * (2 * p), o_sq + 64 * (2 * p + 1)
        cols += [a + r(half), b + r(half), a + half + r(half), b + half + r(half)]
    for g in range(SWA_KV_HEADS):
        a = o_sk + 64 * g
        cols += [a + r(half), a + r(half), a + half + r(half), a + half + r(half)]
    cols += [o_sv + r(128)]
    cols += [_swa_out_order() + o_sz]
    cols += [o_glr + r(GLA_GATE_RANK), np.full(LANES - GLA_GATE_RANK, -1)]
    cols = np.concatenate(cols)
    assert cols.shape == (D_IN_PAD,)
    return cols


def _runs(idx):
    idx = np.asarray(idx)
    runs, start = [], 0
    for end in range(1, len(idx) + 1):
        if end < len(idx) and idx[end] == idx[end - 1] + 1 and idx[end - 1] >= 0:
            continue
        if end < len(idx) and idx[end] < 0 and idx[end - 1] < 0:
            continue
        runs.append((start, int(idx[start]), end - start))
        start = end
    return runs


def _relayout_rows_kernel(w_ref, o_ref, *, runs):
    for dst, src, n in runs:
        o_ref[dst:dst + n, :] = w_ref[src:src + n, :].astype(o_ref.dtype)


def _relayout_rows(w, idx):
    depth, r, c = w.shape
    return pl.pallas_call(
        functools.partial(_relayout_rows_kernel, runs=_runs(idx)),
        grid=(depth,),
        in_specs=[pl.BlockSpec((None, r, c), lambda l: (l, 0, 0))],
        out_specs=pl.BlockSpec((None, len(idx), c), lambda l: (l, 0, 0)),
        out_shape=jax.ShapeDtypeStruct((depth, len(idx), c), BF16),
        compiler_params=pltpu.CompilerParams(vmem_limit_bytes=VMEM_LIMIT_BYTES),
        name="weight_relayout_rows",
    )(w)


def _relayout_t_kernel(wt_ref, o_ref, *, idx, block):
    for j in range(len(idx) // block):
        pieces = []
        for _, src, n in _runs(idx[j * block:(j + 1) * block]):
            pieces.append(jnp.zeros((n, wt_ref.shape[1]), F32) if src < 0 else wt_ref[src:src + n, :])
        o_ref[:, j * block:(j + 1) * block] = jnp.concatenate(pieces, axis=0).T.astype(o_ref.dtype)


def _relayout_t(wt, idx, block):
    depth, c, r = wt.shape
    assert len(idx) % block == 0
    return pl.pallas_call(
        functools.partial(_relayout_t_kernel, idx=np.asarray(idx), block=block),
        grid=(depth,),
        in_specs=[pl.BlockSpec((None, c, r), lambda l: (l, 0, 0))],
        out_specs=pl.BlockSpec((None, r, len(idx)), lambda l: (l, 0, 0)),
        out_shape=jax.ShapeDtypeStruct((depth, r, len(idx)), BF16),
        compiler_params=pltpu.CompilerParams(vmem_limit_bytes=VMEM_LIMIT_BYTES),
        name="weight_relayout_t",
    )(wt)


def _silu(z):
    hz = 0.5 * z
    return hz + hz * jnp.tanh(hz)


def _dot(a, b):
    return jnp.dot(a, b, preferred_element_type=F32)


def _dot_nt(a, b):
    return lax.dot_general(a, b, (((1,), (1,)), ((), ())), preferred_element_type=F32)


def _split_dot(m, x):
    hi = x.astype(BF16)
    lo = (x - hi.astype(F32)).astype(BF16)
    return _dot(m, hi) + _dot(m, lo)


def _split_dot_rhs(x, m):
    hi = x.astype(BF16)
    lo = (x - hi.astype(F32)).astype(BF16)
    return _dot(hi, m) + _dot(lo, m)


def _rope_table_kernel(pos_ref, invf_ref, cos_ref, sin_ref):
    half = HEAD_DIM // 2
    ang = pos_ref[...] * invf_ref[...]
    cos_d, sin_d = jnp.cos(ang), jnp.sin(ang)
    lane = lax.broadcasted_iota(jnp.int32, ang.shape, 1)

    def spread(t, m):
        if m:
            t = pltpu.roll(t, LANES - half * m, 1)
        t = jnp.where(lane < half, t, 0.0)
        t = t + pltpu.roll(t, half, 1)
        return t + pltpu.roll(t, 2 * half, 1)

    for m in range(LANES // half):
        cos_ref[m] = spread(cos_d, m)
        s = spread(sin_d, m)
        sin_ref[m] = jnp.where(lane < LANES // 2, -s, s)


def _layer_kernel(layer_ref, sinks_ref, x_ref, cos_ref, sin_ref, pre_g_ref, w_in_ref, lng_ref, lnb_ref,
                  ws_ref, bs_ref, wg2_ref, bg_ref, gng_ref, w_out_ref, post_g_ref,
                  o_ref, proj_ref, y_ref, state_ref, kprev_ref, vprev_ref,
                  gcum_ref, gk_ref, gv_ref, *, n_sub):
    step = pl.program_id(1)

    @pl.when(step == 0)
    def _():
        state_ref[...] = jnp.zeros_like(state_ref)
        kprev_ref[...] = jnp.zeros_like(kprev_ref)
        vprev_ref[...] = jnp.zeros_like(vprev_ref)

    x = x_ref[...]
    h = x * lax.rsqrt(jnp.mean(x * x, axis=-1, keepdims=True) + NORM_EPS) * pre_g_ref[...]
    proj_ref[...] = _dot(h.astype(BF16), w_in_ref[...])

    def sub_block(i, carry):
        rows = pl.ds(pl.multiple_of(i * BLK, BLK), BLK)
        row = lax.broadcasted_iota(jnp.int32, (BLK, BLK), 0)
        col = lax.broadcasted_iota(jnp.int32, (BLK, BLK), 1)
        lane = lax.broadcasted_iota(jnp.int32, (BLK, LANES), 1)
        n_chunk = BLK // GLA_CHUNK

        def seg(c0, width):
            return proj_ref[rows, c0:c0 + width]

        v = seg(C_GM_V, GM_WIDTH)
        mu = jnp.mean(v, axis=-1, keepdims=True)
        vc = v - mu
        var = jnp.mean(vc * vc, axis=-1, keepdims=True)

        glr = seg(C_GLA_GLR, LANES).astype(BF16)
        logit = _dot(glr, wg2_ref[...]) + bg_ref[...]

        cosb = cos_ref[rows, :]
        sinb = sin_ref[rows, :]

        def rope(t):
            return t * cosb + pltpu.roll(t, LANES // 2, 1) * sinb

        lane2 = lax.broadcasted_iota(jnp.int32, (2 * BLK, LANES), 1)
        k_cur = [rope(seg(C_SWA_K + LANES * g, LANES)).astype(BF16) for g in range(SWA_KV_HEADS)]
        v_cur = seg(C_SWA_V, SWA_KV_WIDTH).astype(BF16)
        k_all = [jnp.concatenate([kprev_ref[g], k_cur[g]], axis=0)
                 for g in range(SWA_KV_HEADS)]
        v_all = jnp.concatenate([vprev_ref[...], v_cur], axis=0)
        one = jnp.ones((), BF16)
        v_aug = [jnp.where(lane2 < HEAD_DIM, v_all, one), jnp.where(lane2 < HEAD_DIM, one, v_all)]
        tq = lax.broadcasted_iota(jnp.int32, (BLK, 2 * BLK), 0)
        kc = lax.broadcasted_iota(jnp.int32, (BLK, 2 * BLK), 1)
        first_block = (step * n_sub + i) == 0
        prev_lo = jnp.where(first_block, BLK, 0)
        in_window = ((kc < BLK) & (kc > tq + prev_lo)) | ((kc >= BLK) & (kc - BLK <= tq))
        bias = jnp.where(in_window, 0.0, -jnp.inf)

        vln = (vc * lax.rsqrt(var + NORM_EPS) * lng_ref[...] + lnb_ref[...]).astype(BF16)
        sv_h = [_dot(jnp.where(col <= row, ws_ref[hh], 0.0).astype(BF16), vln) for hh in range(GM_HEADS)]

        log_a = (jnp.minimum(logit, 0.0) - jnp.log1p(jnp.exp(-jnp.abs(logit)))) * (LOG2E / GLA_TAU)
        same_chunk = (row // GLA_CHUNK) == (col // GLA_CHUNK)
        tri16 = jnp.where(same_chunk & (col <= row), 1.0, 0.0).astype(BF16)
        ones16 = jnp.where(same_chunk, 1.0, 0.0).astype(BF16)
        g_cum = _split_dot(tri16, log_a)
        g_end = _split_dot(ones16, log_a)

        q_pairs = [rope(seg(C_SWA_Q + LANES * p, LANES)) * (HEAD_DIM ** -0.5 * LOG2E)
                   for p in range(SWA_Q_HEADS // 2)]
        lane_parity = (lane // (HEAD_DIM // 2)) % 2

        lane_head = lax.broadcasted_iota(jnp.int32, (BLK, GM_WIDTH), 1) // HEAD_DIM
        sv = sv_h[0]
        for hh in range(1, GM_HEADS):
            sv = jnp.where(lane_head == hh, sv_h[hh], sv)
        ya = seg(C_GM_U, GM_WIDTH) * (sv + bs_ref[...]) * _silu(seg(C_GM_Z, GM_WIDTH))
        y_ref[rows, Y_A:Y_A + GM_WIDTH] = ya.astype(BF16)

        q = seg(C_GLA_Q, GLA_KEY_WIDTH) * (GLA_DK ** -0.5)
        k = seg(C_GLA_K, GLA_KEY_WIDTH)
        gv = seg(C_GLA_V, GLA_WIDTH)
        gcum_ref[...] = g_cum
        gk_ref[...] = k
        gv_ref[...] = gv
        hk = lax.broadcasted_iota(jnp.int32, (GLA_KEY_WIDTH, GLA_WIDTH), 0) // GLA_DK
        hv = lax.broadcasted_iota(jnp.int32, (GLA_KEY_WIDTH, GLA_WIDTH), 1) // GLA_DV
        head_match = hk == hv
        expand = jnp.where(head_match, 1.0, 0.0).astype(BF16)

        def state_updates():
            k_dec = k * jnp.exp2(g_end - g_cum)
            k_dec_t = k_dec.T
            lhs = jnp.concatenate(
                [jnp.where(col // GLA_CHUNK == jj, k_dec_t, 0.0) for jj in range(n_chunk)], axis=0)
            return _dot(lhs.astype(BF16), gv.astype(BF16))

        upd = state_updates()
        q_dec = q * jnp.exp2(g_cum)
        a_t = jnp.exp2(g_end).T

        t_local = row % GLA_CHUNK

        def chunk_rows(ref, s, n):
            return jnp.concatenate(
                [jnp.broadcast_to(ref[pl.ds(c * GLA_CHUNK + s, 1), :], (n, ref.shape[1]))
                 for c in range(n_chunk)], axis=0)

        def upper_half(t):
            return t.reshape(n_chunk, 2, SUBLANES, t.shape[1])[:, 1].reshape(BLK // 2, t.shape[1])

        q_up, g_up, t_up = upper_half(q), upper_half(g_cum), upper_half(t_local)

        def intra_term(s):
            if s < SUBLANES:
                qq, gg, tt, n = q, g_cum, t_local, GLA_CHUNK
            else:
                qq, gg, tt, n = q_up, g_up, t_up, SUBLANES
            term = qq * chunk_rows(gk_ref, s, n) * jnp.exp2(gg - chunk_rows(gcum_ref, s, n))
            term = jnp.where(tt >= s, term, 0.0)
            return _dot(term.astype(BF16), expand) * chunk_rows(gv_ref, s, n)

        def attention_head(hh):
            p, par = divmod(hh, 2)
            g = hh // SWA_GROUP
            qm = jnp.where(lane_parity == par, q_pairs[p], 0.0).astype(BF16)
            sc = _dot_nt(qm, k_all[g]) + bias
            sink = sinks_ref[layer_ref[0], hh] * LOG2E
            m = jnp.maximum(jnp.max(sc, axis=-1, keepdims=True), sink)
            return _dot(jnp.exp2(sc - m).astype(BF16), v_aug[g]), jnp.exp2(sink - m)

        o_lo = jnp.zeros((BLK, GLA_WIDTH), F32)
        o_up = jnp.zeros((BLK // 2, GLA_WIDTH), F32)
        state = state_ref[...]
        outs, pv, sink_w = [], [None] * SWA_Q_HEADS, [None] * SWA_Q_HEADS
        for jj in range(n_chunk):
            o_lo = o_lo + intra_term(jj)
            o_up = o_up + intra_term(SUBLANES + jj)
            pv[jj], sink_w[jj] = attention_head(jj)
            qj = q_dec[jj * GLA_CHUNK:(jj + 1) * GLA_CHUNK].astype(BF16)
            outs.append(_dot(qj, state.astype(BF16)))
            a_j = a_t[:, jj * GLA_CHUNK:jj * GLA_CHUNK + 1]
            state = a_j * state + jnp.where(head_match, upd[jj * BLK:(jj + 1) * BLK], 0.0)
        state_ref[...] = state

        o_up = jnp.concatenate(
            [jnp.zeros((n_chunk, 1, SUBLANES, GLA_WIDTH), F32),
             o_up.reshape(n_chunk, 1, SUBLANES, GLA_WIDTH)], axis=1).reshape(BLK, GLA_WIDTH)
        o_gla = o_lo + o_up + jnp.concatenate(outs, axis=0)
        seg_r = lax.broadcasted_iota(jnp.int32, (GLA_WIDTH, GLA_WIDTH), 0) // GLA_DV
        seg_c = lax.broadcasted_iota(jnp.int32, (GLA_WIDTH, GLA_WIDTH), 1) // GLA_DV
        head_ones = jnp.where(seg_r == seg_c, 1.0, 0.0).astype(BF16)
        ms = _split_dot_rhs(o_gla * o_gla, head_ones) * (1.0 / GLA_DV)

        for c in range(SWA_GROUP):
            lo_head = lane < HEAD_DIM
            num = jnp.where(lo_head, pv[c], pv[SWA_GROUP + c])
            den = pltpu.roll(jnp.where(lo_head, pv[SWA_GROUP + c], pv[c]), LANES // 2, 1)
            den = den + jnp.where(lo_head, sink_w[c], sink_w[SWA_GROUP + c])
            yc = num / den * _silu(seg(C_SWA_Z + LANES * c, LANES))
            y_ref[rows, Y_C + LANES * c:Y_C + LANES * (c + 1)] = yc.astype(BF16)
        for g in range(SWA_KV_HEADS):
            kprev_ref[g] = k_cur[g]
        vprev_ref[...] = v_cur

        yb = o_gla * lax.rsqrt(ms + NORM_EPS) * gng_ref[...]
        yb = yb * _silu(seg(C_GLA_Z, GLA_WIDTH))
        y_ref[rows, Y_B:Y_B + GLA_WIDTH] = yb.astype(BF16)
        return carry

    lax.fori_loop(0, n_sub, sub_block, 0)

    y2 = _dot(y_ref[...], w_out_ref[...])
    y2 = y2 * lax.rsqrt(jnp.mean(y2 * y2, axis=-1, keepdims=True) + NORM_EPS) * post_g_ref[...]
    o_ref[...] = x_ref[...] + y2


def _rope_tables(positions):
    n = positions.size
    half = HEAD_DIM // 2
    inv_freq = jnp.power(ROPE_THETA, -jnp.arange(half, dtype=F32) * 2.0 / HEAD_DIM)
    per_row = LANES // half
    quarter = n // per_row
    invf = jnp.tile(inv_freq, per_row).reshape(1, LANES)
    pos = jnp.repeat(positions.reshape(per_row, quarter).T.astype(F32), half, axis=1)
    rows = 1024
    assert quarter % rows == 0
    out_spec = pl.BlockSpec((per_row, rows, LANES), lambda i: (0, i, 0))
    cos_t, sin_t = pl.pallas_call(
        _rope_table_kernel,
        grid=(quarter // rows,),
        in_specs=[pl.BlockSpec((rows, LANES), lambda i: (i, 0)), pl.BlockSpec((1, LANES), lambda i: (0, 0))],
        out_specs=[out_spec, out_spec],
        out_shape=[jax.ShapeDtypeStruct((per_row, quarter, LANES), F32)] * 2,
        name="rope_tables",
    )(pos, invf)
    return cos_t.reshape(n, LANES), sin_t.reshape(n, LANES)


def _layer_call(batch, seq):
    tb = TOKEN_BLOCK
    n_sub = tb // BLK
    steps = seq // tb
    tok = lambda b, j, l: (b * steps + j, 0)
    layer3 = lambda b, j, l: (l[0], 0, 0)
    layer4 = lambda b, j, l: (l[0], 0, 0, 0)
    in_specs = [
        pl.BlockSpec(memory_space=pltpu.SMEM),
        pl.BlockSpec((tb, D_MODEL), tok),
        pl.BlockSpec((tb, LANES), tok),
        pl.BlockSpec((tb, LANES), tok),
        pl.BlockSpec((None, 1, D_MODEL), layer3),
        pl.BlockSpec((None, D_MODEL, D_IN_PAD), layer3),
        pl.BlockSpec((None, 1, GM_WIDTH), layer3),
        pl.BlockSpec((None, 1, GM_WIDTH), layer3),
        pl.BlockSpec((None, GM_HEADS, BLK, BLK), layer4),
        pl.BlockSpec((None, BLK, GM_WIDTH), layer3),
        pl.BlockSpec((None, LANES, GLA_KEY_WIDTH), layer3),
        pl.BlockSpec((None, 1, GLA_KEY_WIDTH), layer3),
        pl.BlockSpec((None, 1, GLA_WIDTH), layer3),
        pl.BlockSpec((None, D_MIX, D_MODEL), layer3),
        pl.BlockSpec((None, 1, D_MODEL), layer3),
    ]
    return pl.pallas_call(
        functools.partial(_layer_kernel, n_sub=n_sub),
        grid_spec=pltpu.PrefetchScalarGridSpec(
            num_scalar_prefetch=1,
            grid=(batch, steps),
            in_specs=in_specs,
            out_specs=pl.BlockSpec((tb, D_MODEL), tok),
            scratch_shapes=[
                pltpu.VMEM((tb, D_IN_PAD), F32),
                pltpu.VMEM((tb, D_MIX), BF16),
                pltpu.VMEM((GLA_KEY_WIDTH, GLA_WIDTH), F32),
                pltpu.VMEM((SWA_KV_HEADS, BLK, LANES), BF16),
                pltpu.VMEM((BLK, SWA_KV_WIDTH), BF16),
                pltpu.VMEM((BLK, GLA_KEY_WIDTH), F32),
                pltpu.VMEM((BLK, GLA_KEY_WIDTH), F32),
                pltpu.VMEM((BLK, GLA_WIDTH), F32),
            ]),
        out_shape=jax.ShapeDtypeStruct((batch * seq, D_MODEL), F32),
        compiler_params=pltpu.CompilerParams(
            dimension_semantics=("arbitrary", "arbitrary"),
            vmem_limit_bytes=VMEM_LIMIT_BYTES),
        name="hybrid_layer",
    )


def kernel(x, positions, pre_g, w_in, gm_ln_g, gm_ln_b, gm_ws, gm_bs, gla_wg2, gla_bg, gla_norm_g,
           swa_sinks, w_out, post_g):
    batch, seq, d_model = x.shape
    assert d_model == D_MODEL and seq % TOKEN_BLOCK == 0
    assert w_in.shape == (DEPTH, D_MODEL, D_IN) and w_out.shape == (DEPTH, D_MIX, D_MODEL)

    row = lambda p: p[:, None, :]
    w_in_p = _relayout_t(jnp.swapaxes(w_in, 1, 2), _in_proj_columns(), block=2 * LANES)
    out_rows = np.concatenate([np.arange(Y_C), Y_C + _swa_out_order()])
    w_out_p = _relayout_rows(w_out, out_rows)
    bs_p = jnp.repeat(jnp.swapaxes(gm_bs, 1, 2), HEAD_DIM, axis=2)
    wg2_p = jnp.pad(gla_wg2, ((0, 0), (0, LANES - GLA_GATE_RANK), (0, 0))).astype(BF16)
    gng_p = jnp.tile(gla_norm_g, (1, GLA_HEADS))

    cos_t, sin_t = _rope_tables(positions)
    layer = _layer_call(batch, seq)
    h = x.reshape(batch * seq, D_MODEL)
    for l in range(DEPTH):
        h = layer(jnp.full((1,), l, jnp.int32), swa_sinks, h, cos_t, sin_t, row(pre_g), w_in_p,
                  row(gm_ln_g), row(gm_ln_b), gm_ws, bs_p, wg2_p, row(gla_bg), row(gng_p),
                  w_out_p, row(post_g))
    return h.reshape(batch, seq, D_MODEL)
```

```python
import functools
import math

import numpy as np
import jax
import jax.numpy as jnp
from jax import lax
from jax.experimental import pallas as pl
from jax.experimental.pallas import tpu as pltpu

F32 = jnp.float32
BF16 = jnp.bfloat16

D_MODEL = 1024
DEPTH = 4
HEAD_DIM = 64
NORM_EPS = 1e-6
GM_HEADS = 4
GM_WIDTH = 256
GLA_HEADS = 4
GLA_DV = 64
GLA_DK = 32
GLA_WIDTH = 256
GLA_KEY_WIDTH = 128
GLA_GATE_RANK = 16
GLA_TAU = 16.0
GLA_CHUNK = 16
SWA_Q_HEADS = 8
SWA_KV_HEADS = 2
SWA_GROUP = SWA_Q_HEADS // SWA_KV_HEADS
SWA_WIDTH = 512
SWA_KV_WIDTH = 128
ROPE_THETA = 10000.0
D_MIX = 1024
D_IN = 2832
LOG2E = math.log2(math.e)

BLK = 128
LANES = 128
SUBLANES = 8
TOKEN_BLOCK = 1024
VMEM_LIMIT_BYTES = 56 * 1024 * 1024

C_GM_U, C_GM_V, C_GM_Z = 0, 256, 512
C_GLA_Q, C_GLA_K, C_GLA_V, C_GLA_Z = 768, 896, 1024, 1280
C_SWA_Q, C_SWA_K, C_SWA_V, C_SWA_Z = 1536, 2048, 2304, 2432
C_GLA_GLR = 2944
D_IN_PAD = 3072
Y_A, Y_B, Y_C = 0, 256, 512


def _swa_out_order():
    r = np.arange(HEAD_DIM)
    return np.concatenate(
        [np.concatenate([64 * c + r, 64 * (SWA_GROUP + c) + r]) for c in range(SWA_GROUP)])


def _in_proj_columns():
    o = np.cumsum((0, 256, 256, 256, 128, 128, 256, 16, 256, 512, 128, 128, 512))
    (o_u, o_v, o_z, o_q, o_k, o_gv, o_glr, o_gz, o_sq, o_sk, o_sv, o_sz, _) = o
    half = HEAD_DIM // 2
    r = np.arange
    cols = [o_u + r(256), o_v + r(256), o_z + r(256),
            o_q + r(128), o_k + r(128), o_gv + r(256), o_gz + r(256)]
    for p in range(SWA_Q_HEADS // 2):
        a, b = o_sq + 64 * (2 * p), o_sq + 64 * (2 * p + 1)
        cols += [a + r(half), b + r(half), a + half + r(half), b + half + r(half)]
    for g in range(SWA_KV_HEADS):
        a = o_sk + 64 * g
        cols += [a + r(half), a + r(half), a + half + r(half), a + half + r(half)]
    cols += [o_sv + r(128)]
    cols += [_swa_out_order() + o_sz]
    cols += [o_glr + r(GLA_GATE_RANK), np.full(LANES - GLA_GATE_RANK, -1)]
    cols = np.concatenate(cols)
    assert cols.shape == (D_IN_PAD,)
    return cols


def _runs(idx):
    idx = np.asarray(idx)
    runs, start = [], 0
    for end in range(1, len(idx) + 1):
        if end < len(idx) and idx[end] == idx[end - 1] + 1 and idx[end - 1] >= 0:
            continue
        if end < len(idx) and idx[end] < 0 and idx[end - 1] < 0:
            continue
        runs.append((start, int(idx[start]), end - start))
        start = end
    return runs


def _relayout_rows_kernel(w_ref, o_ref, *, runs):
    for dst, src, n in runs:
        o_ref[dst:dst + n, :] = w_ref[src:src + n, :].astype(o_ref.dtype)


def _relayout_rows(w, idx):
    depth, r, c = w.shape
    return pl.pallas_call(
        functools.partial(_relayout_rows_kernel, runs=_runs(idx)),
        grid=(depth,),
        in_specs=[pl.BlockSpec((None, r, c), lambda l: (l, 0, 0))],
        out_specs=pl.BlockSpec((None, len(idx), c), lambda l: (l, 0, 0)),
        out_shape=jax.ShapeDtypeStruct((depth, len(idx), c), BF16),
        compiler_params=pltpu.CompilerParams(vmem_limit_bytes=VMEM_LIMIT_BYTES),
        name="weight_relayout_rows",
    )(w)


def _relayout_t_kernel(wt_ref, o_ref, *, idx, block):
    for j in range(len(idx) // block):
        pieces = []
        for _, src, n in _runs(idx[j * block:(j + 1) * block]):
            pieces.append(jnp.zeros((n, wt_ref.shape[1]), F32) if src < 0 else wt_ref[src:src + n, :])
        o_ref[:, j * block:(j + 1) * block] = jnp.concatenate(pieces, axis=0).T.astype(o_ref.dtype)


def _relayout_t(wt, idx, block):
    depth, c, r = wt.shape
    assert len(idx) % block == 0
    return pl.pallas_call(
        functools.partial(_relayout_t_kernel, idx=np.asarray(idx), block=block),
        grid=(depth,),
        in_specs=[pl.BlockSpec((None, c, r), lambda l: (l, 0, 0))],
        out_specs=pl.BlockSpec((None, r, len(idx)), lambda l: (l, 0, 0)),
        out_shape=jax.ShapeDtypeStruct((depth, r, len(idx)), BF16),
        compiler_params=pltpu.CompilerParams(vmem_limit_bytes=VMEM_LIMIT_BYTES),
        name="weight_relayout_t",
    )(wt)


def _silu(z):
    hz = 0.5 * z
    return hz + hz * jnp.tanh(hz)


def _dot(a, b):
    return jnp.dot(a, b, preferred_element_type=F32)


def _dot_nt(a, b):
    return lax.dot_general(a, b, (((1,), (1,)), ((), ())), preferred_element_type=F32)


def _rope_table_kernel(pos_ref, invf_ref, cos_ref, sin_ref):
    half = HEAD_DIM // 2
    ang = pos_ref[...] * invf_ref[...]
    cos_d, sin_d = jnp.cos(ang), jnp.sin(ang)
    lane = lax.broadcasted_iota(jnp.int32, ang.shape, 1)

    def spread(t, m):
        if m:
            t = pltpu.roll(t, LANES - half * m, 1)
        t = jnp.where(lane < half, t, 0.0)
        t = t + pltpu.roll(t, half, 1)
        return t + pltpu.roll(t, 2 * half, 1)

    for m in range(LANES // half):
        cos_ref[m] = spread(cos_d, m)
        s = spread(sin_d, m)
        sin_ref[m] = jnp.where(lane < LANES // 2, -s, s)


def _layer_kernel(layer_ref, sinks_ref, x_ref, cos_ref, sin_ref, pre_g_ref, w_in_ref, lng_ref, lnb_ref,
                  ws_ref, bs_ref, wg2_ref, bg_ref, gng_ref, w_out_ref, post_g_ref,
                  o_ref, proj_ref, y_ref, state_ref, kprev_ref, vprev_ref,
                  gcum_ref, gk_ref, gv_ref, *, n_sub):
    step = pl.program_id(1)

    @pl.when(step == 0)
    def _():
        state_ref[...] = jnp.zeros_like(state_ref)
        kprev_ref[...] = jnp.zeros_like(kprev_ref)
        vprev_ref[...] = jnp.zeros_like(vprev_ref)

    x = x_ref[...]
    h = x * lax.rsqrt(jnp.mean(x * x, axis=-1, keepdims=True) + NORM_EPS) * pre_g_ref[...]
    proj_ref[...] = _dot(h.astype(BF16), w_in_ref[...])

    def sub_block(i, carry):
        rows = pl.ds(pl.multiple_of(i * BLK, BLK), BLK)
        row = lax.broadcasted_iota(jnp.int32, (BLK, BLK), 0)
        col = lax.broadcasted_iota(jnp.int32, (BLK, BLK), 1)
        lane = lax.broadcasted_iota(jnp.int32, (BLK, LANES), 1)
        n_chunk = BLK // GLA_CHUNK

        def seg(c0, width):
            return proj_ref[rows, c0:c0 + width]

        v = seg(C_GM_V, GM_WIDTH)
        mu = jnp.mean(v, axis=-1, keepdims=True)
        vc = v - mu
        var = jnp.mean(vc * vc, axis=-1, keepdims=True)

        glr = seg(C_GLA_GLR, LANES).astype(BF16)
        logit = _dot(glr, wg2_ref[...]) + bg_ref[...]

        cosb = cos_ref[rows, :]
        sinb = sin_ref[rows, :]

        def rope(t):
            return t * cosb + pltpu.roll(t, LANES // 2, 1) * sinb

        lane2 = lax.broadcasted_iota(jnp.int32, (2 * BLK, LANES), 1)
        k_cur = [rope(seg(C_SWA_K + LANES * g, LANES)).astype(BF16) for g in range(SWA_KV_HEADS)]
        v_cur = seg(C_SWA_V, SWA_KV_WIDTH).astype(BF16)
        k_all = [jnp.concatenate([kprev_ref[g], k_cur[g]], axis=0)
                 for g in range(SWA_KV_HEADS)]
        v_all = jnp.concatenate([vprev_ref[...], v_cur], axis=0)
        one = jnp.ones((), BF16)
        v_aug = [jnp.where(lane2 < HEAD_DIM, v_all, one), jnp.where(lane2 < HEAD_DIM, one, v_all)]
        tq = lax.broadcasted_iota(jnp.int32, (BLK, 2 * BLK), 0)
        kc = lax.broadcasted_iota(jnp.int32, (BLK, 2 * BLK), 1)
        first_block = (step * n_sub + i) == 0
        prev_lo = jnp.where(first_block, BLK, 0)
        in_window = ((kc < BLK) & (kc > tq + prev_lo)) | ((kc >= BLK) & (kc - BLK <= tq))
        bias = jnp.where(in_window, 0.0, -jnp.inf)

        vln = (vc * lax.rsqrt(var + NORM_EPS) * lng_ref[...] + lnb_ref[...]).astype(BF16)
        w_tril = jnp.concatenate(
            [jnp.where(col <= row, ws_ref[hh], 0.0) for hh in range(GM_HEADS)], axis=0).astype(BF16)
        sv_all = _dot(w_tril, vln)
        sv_h = [sv_all[hh * BLK:(hh + 1) * BLK] for hh in range(GM_HEADS)]

        log_a = (jnp.minimum(logit, 0.0) - jnp.log1p(jnp.exp(-jnp.abs(logit)))) * (LOG2E / GLA_TAU)
        same_chunk = (row // GLA_CHUNK) == (col // GLA_CHUNK)
        tri16 = jnp.where(same_chunk & (col <= row), 1.0, 0.0).astype(BF16)
        ones16 = jnp.where(same_chunk, 1.0, 0.0).astype(BF16)
        la_hi = log_a.astype(BF16)
        la_lo = (log_a - la_hi.astype(F32)).astype(BF16)
        sums = _dot(jnp.concatenate([tri16, ones16], axis=0), jnp.concatenate([la_hi, la_lo], axis=1))
        g_cum = sums[:BLK, :LANES] + sums[:BLK, LANES:]
        g_end = sums[BLK:, :LANES] + sums[BLK:, LANES:]

        q_pairs = [rope(seg(C_SWA_Q + LANES * p, LANES)) * (HEAD_DIM ** -0.5 * LOG2E)
                   for p in range(SWA_Q_HEADS // 2)]
        lane_parity = (lane // (HEAD_DIM // 2)) % 2

        lane_head = lax.broadcasted_iota(jnp.int32, (BLK, GM_WIDTH), 1) // HEAD_DIM
        sv = sv_h[0]
        for hh in range(1, GM_HEADS):
            sv = jnp.where(lane_head == hh, sv_h[hh], sv)
        ya = seg(C_GM_U, GM_WIDTH) * (sv + bs_ref[...]) * _silu(seg(C_GM_Z, GM_WIDTH))
        y_ref[rows, Y_A:Y_A + GM_WIDTH] = ya.astype(BF16)

        q = seg(C_GLA_Q, GLA_KEY_WIDTH) * (GLA_DK ** -0.5)
        k = seg(C_GLA_K, GLA_KEY_WIDTH)
        gv = seg(C_GLA_V, GLA_WIDTH)
        gcum_ref[...] = g_cum
        gk_ref[...] = k
        gv_ref[...] = gv
        hk = lax.broadcasted_iota(jnp.int32, (GLA_KEY_WIDTH, GLA_WIDTH), 0) // GLA_DK
        hv = lax.broadcasted_iota(jnp.int32, (GLA_KEY_WIDTH, GLA_WIDTH), 1) // GLA_DV
        head_match = hk == hv
        expand = jnp.where(head_match, 1.0, 0.0).astype(BF16)

        def state_updates():
            k_dec = k * jnp.exp2(g_end - g_cum)
            k_dec_t = k_dec.T
            lhs = jnp.concatenate(
                [jnp.where(col // GLA_CHUNK == jj, k_dec_t, 0.0) for jj in range(n_chunk)], axis=0)
            return _dot(lhs.astype(BF16), gv.astype(BF16))

        upd = state_updates()
        q_dec = q * jnp.exp2(g_cum)
        a_t = jnp.exp2(g_end).T

        t_local = row % GLA_CHUNK

        def chunk_rows(ref, s, n):
            return jnp.concatenate(
                [jnp.broadcast_to(ref[pl.ds(c * GLA_CHUNK + s, 1), :], (n, ref.shape[1]))
                 for c in range(n_chunk)], axis=0)

        def upper_half(t):
            return t.reshape(n_chunk, 2, SUBLANES, t.shape[1])[:, 1].reshape(BLK // 2, t.shape[1])

        q_up, g_up, t_up = upper_half(q), upper_half(g_cum), upper_half(t_local)

        def intra_terms(s_list):
            upper = s_list[0] >= SUBLANES
            qq, gg, tt, n = (q_up, g_up, t_up, SUBLANES) if upper else (q, g_cum, t_local, GLA_CHUNK)
            terms = []
            for s in s_list:
                term = qq * chunk_rows(gk_ref, s, n) * jnp.exp2(gg - chunk_rows(gcum_ref, s, n))
                terms.append(jnp.where(tt >= s, term, 0.0).astype(BF16))
            spread = _dot(jnp.concatenate(terms, axis=0), expand)
            m = qq.shape[0]
            out = spread[0:m] * chunk_rows(gv_ref, s_list[0], n)
            for i, s in enumerate(s_list[1:], 1):
                out = out + spread[i * m:(i + 1) * m] * chunk_rows(gv_ref, s, n)
            return out

        def attention_group(g):
            qm = []
            for hh in range(g * SWA_GROUP, (g + 1) * SWA_GROUP):
                p, par = divmod(hh, 2)
                qm.append(jnp.where(lane_parity == par, q_pairs[p], 0.0).astype(BF16))
            sc = _dot_nt(jnp.concatenate(qm, axis=0), k_all[g])
            es, sinks_w = [], []
            for i, hh in enumerate(range(g * SWA_GROUP, (g + 1) * SWA_GROUP)):
                sc_h = sc[i * BLK:(i + 1) * BLK] + bias
                sink = sinks_ref[layer_ref[0], hh] * LOG2E
                m = jnp.maximum(jnp.max(sc_h, axis=-1, keepdims=True), sink)
                es.append(jnp.exp2(sc_h - m).astype(BF16))
                sinks_w.append(jnp.exp2(sink - m))
            pv_all = _dot(jnp.concatenate(es, axis=0), v_aug[g])
            return [pv_all[i * BLK:(i + 1) * BLK] for i in range(SWA_GROUP)], sinks_w

        half = SUBLANES // 2
        o_lo = intra_terms(list(range(0, half)))
        pv, sink_w = attention_group(0)
        o_lo = o_lo + intra_terms(list(range(half, SUBLANES)))
        o_up = intra_terms(list(range(SUBLANES, SUBLANES + half)))
        pv1, sink_w1 = attention_group(1)
        pv, sink_w = pv + pv1, sink_w + sink_w1
        o_up = o_up + intra_terms(list(range(SUBLANES + half, GLA_CHUNK)))

        state = state_ref[...]
        outs = []
        for jj in range(n_chunk):
            qj = q_dec[jj * GLA_CHUNK:(jj + 1) * GLA_CHUNK].astype(BF16)
            outs.append(_dot(qj, state.astype(BF16)))
            a_j = a_t[:, jj * GLA_CHUNK:jj * GLA_CHUNK + 1]
            state = a_j * state + jnp.where(head_match, upd[jj * BLK:(jj + 1) * BLK], 0.0)
        state_ref[...] = state

        o_up = jnp.concatenate(
            [jnp.zeros((n_chunk, 1, SUBLANES, GLA_WIDTH), F32),
             o_up.reshape(n_chunk, 1, SUBLANES, GLA_WIDTH)], axis=1).reshape(BLK, GLA_WIDTH)
        o_gla = o_lo + o_up + jnp.concatenate(outs, axis=0)
        seg_r = lax.broadcasted_iota(jnp.int32, (GLA_WIDTH, GLA_WIDTH), 0) // GLA_DV
        seg_c = lax.broadcasted_iota(jnp.int32, (GLA_WIDTH, GLA_WIDTH), 1) // GLA_DV
        head_ones = jnp.where(seg_r == seg_c, 1.0, 0.0).astype(BF16)
        sq = o_gla * o_gla
        sq_hi = sq.astype(BF16)
        sq_lo = (sq - sq_hi.astype(F32)).astype(BF16)
        sums = _dot(jnp.concatenate([sq_hi, sq_lo], axis=0), head_ones)
        ms = (sums[:BLK] + sums[BLK:]) * (1.0 / GLA_DV)

        for c in range(SWA_GROUP):
            lo_head = lane < HEAD_DIM
            num = jnp.where(lo_head, pv[c], pv[SWA_GROUP + c])
            den = pltpu.roll(jnp.where(lo_head, pv[SWA_GROUP + c], pv[c]), LANES // 2, 1)
            den = den + jnp.where(lo_head, sink_w[c], sink_w[SWA_GROUP + c])
            yc = num / den * _silu(seg(C_SWA_Z + LANES * c, LANES))
            y_ref[rows, Y_C + LANES * c:Y_C + LANES * (c + 1)] = yc.astype(BF16)
        for g in range(SWA_KV_HEADS):
            kprev_ref[g] = k_cur[g]
        vprev_ref[...] = v_cur

        yb = o_gla * lax.rsqrt(ms + NORM_EPS) * gng_ref[...]
        yb = yb * _silu(seg(C_GLA_Z, GLA_WIDTH))
        y_ref[rows, Y_B:Y_B + GLA_WIDTH] = yb.astype(BF16)
        return carry

    lax.fori_loop(0, n_sub, sub_block, 0)

    y2 = _dot(y_ref[...], w_out_ref[...])
    y2 = y2 * lax.rsqrt(jnp.mean(y2 * y2, axis=-1, keepdims=True) + NORM_EPS) * post_g_ref[...]
    o_ref[...] = x_ref[...] + y2


def _rope_tables(positions):
    n = positions.size
    half = HEAD_DIM // 2
    inv_freq = jnp.power(ROPE_THETA, -jnp.arange(half, dtype=F32) * 2.0 / HEAD_DIM)
    per_row = LANES // half
    quarter = n // per_row
    invf = jnp.tile(inv_freq, per_row).reshape(1, LANES)
    pos = jnp.repeat(positions.reshape(per_row, quarter).T.astype(F32), half, axis=1)
    rows = 1024
    assert quarter % rows == 0
    out_spec = pl.BlockSpec((per_row, rows, LANES), lambda i: (0, i, 0))
    cos_t, sin_t = pl.pallas_call(
        _rope_table_kernel,
        grid=(quarter // rows,),
        in_specs=[pl.BlockSpec((rows, LANES), lambda i: (i, 0)), pl.BlockSpec((1, LANES), lambda i: (0, 0))],
        out_specs=[out_spec, out_spec],
        out_shape=[jax.ShapeDtypeStruct((per_row, quarter, LANES), F32)] * 2,
        name="rope_tables",
    )(pos, invf)
    return cos_t.reshape(n, LANES), sin_t.reshape(n, LANES)


def _layer_call(batch, seq):
    tb = TOKEN_BLOCK
    n_sub = tb // BLK
    steps = seq // tb
    tok = lambda b, j, l: (b * steps + j, 0)
    layer3 = lambda b, j, l: (l[0], 0, 0)
    layer4 = lambda b, j, l: (l[0], 0, 0, 0)
    in_specs = [
        pl.BlockSpec(memory_space=pltpu.SMEM),
        pl.BlockSpec((tb, D_MODEL), tok),
        pl.BlockSpec((tb, LANES), tok),
        pl.BlockSpec((tb, LANES), tok),
        pl.BlockSpec((None, 1, D_MODEL), layer3),
        pl.BlockSpec((None, D_MODEL, D_IN_PAD), layer3),
        pl.BlockSpec((None, 1, GM_WIDTH), layer3),
        pl.BlockSpec((None, 1, GM_WIDTH), layer3),
        pl.BlockSpec((None, GM_HEADS, BLK, BLK), layer4),
        pl.BlockSpec((None, BLK, GM_WIDTH), layer3),
        pl.BlockSpec((None, LANES, GLA_KEY_WIDTH), layer3),
        pl.BlockSpec((None, 1, GLA_KEY_WIDTH), layer3),
        pl.BlockSpec((None, 1, GLA_WIDTH), layer3),
        pl.BlockSpec((None, D_MIX, D_MODEL), layer3),
        pl.BlockSpec((None, 1, D_MODEL), layer3),
    ]
    return pl.pallas_call(
        functools.partial(_layer_kernel, n_sub=n_sub),
        grid_spec=pltpu.PrefetchScalarGridSpec(
            num_scalar_prefetch=1,
            grid=(batch, steps),
            in_specs=in_specs,
            out_specs=pl.BlockSpec((tb, D_MODEL), tok),
            scratch_shapes=[
                pltpu.VMEM((tb, D_IN_PAD), F32),
                pltpu.VMEM((tb, D_MIX), BF16),
                pltpu.VMEM((GLA_KEY_WIDTH, GLA_WIDTH), F32),
                pltpu.VMEM((SWA_KV_HEADS, BLK, LANES), BF16),
                pltpu.VMEM((BLK, SWA_KV_WIDTH), BF16),
                pltpu.VMEM((BLK, GLA_KEY_WIDTH), F32),
                pltpu.VMEM((BLK, GLA_KEY_WIDTH), F32),
                pltpu.VMEM((BLK, GLA_WIDTH), F32),
            ]),
        out_shape=jax.ShapeDtypeStruct((batch * seq, D_MODEL), F32),
        compiler_params=pltpu.CompilerParams(
            dimension_semantics=("arbitrary", "arbitrary"),
            vmem_limit_bytes=VMEM_LIMIT_BYTES),
        name="hybrid_layer",
    )


def kernel(x, positions, pre_g, w_in, gm_ln_g, gm_ln_b, gm_ws, gm_bs, gla_wg2, gla_bg, gla_norm_g,
           swa_sinks, w_out, post_g):
    batch, seq, d_model = x.shape
    assert d_model == D_MODEL and seq % TOKEN_BLOCK == 0
    assert w_in.shape == (DEPTH, D_MODEL, D_IN) and w_out.shape == (DEPTH, D_MIX, D_MODEL)

    row = lambda p: p[:, None, :]
    w_in_p = _relayout_t(jnp.swapaxes(w_in, 1, 2), _in_proj_columns(), block=2 * LANES)
    out_rows = np.concatenate([np.arange(Y_C), Y_C + _swa_out_order()])
    w_out_p = _relayout_rows(w_out, out_rows)
    bs_p = jnp.repeat(jnp.swapaxes(gm_bs, 1, 2), HEAD_DIM, axis=2)
    wg2_p = jnp.pad(gla_wg2, ((0, 0), (0, LANES - GLA_GATE_RANK), (0, 0))).astype(BF16)
    gng_p = jnp.tile(gla_norm_g, (1, GLA_HEADS))

    cos_t, sin_t = _rope_tables(positions)
    layer = _layer_call(batch, seq)
    h = x.reshape(batch * seq, D_MODEL)
    for l in range(DEPTH):
        h = layer(jnp.full((1,), l, jnp.int32), swa_sinks, h, cos_t, sin_t, row(pre_g), w_in_p,
                  row(gm_ln_g), row(gm_ln_b), gm_ws, bs_p, wg2_p, row(gla_bg), row(gng_p),
                  w_out_p, row(post_g))
    return h.reshape(batch, seq, D_MODEL)
```

```python
import functools
import math

import numpy as np
import jax
import jax.numpy as jnp
from jax import lax
from jax.experimental import pallas as pl
from jax.experimental.pallas import tpu as pltpu

F32 = jnp.float32
BF16 = jnp.bfloat16

D_MODEL = 1024
DEPTH = 4
HEAD_DIM = 64
NORM_EPS = 1e-6
GM_HEADS = 4
GM_WIDTH = 256
GLA_HEADS = 4
GLA_DV = 64
GLA_DK = 32
GLA_WIDTH = 256
GLA_KEY_WIDTH = 128
GLA_GATE_RANK = 16
GLA_TAU = 16.0
GLA_CHUNK = 16
SWA_Q_HEADS = 8
SWA_KV_HEADS = 2
SWA_GROUP = SWA_Q_HEADS // SWA_KV_HEADS
SWA_WIDTH = 512
SWA_KV_WIDTH = 128
ROPE_THETA = 10000.0
D_MIX = 1024
D_IN = 2832
LOG2E = math.log2(math.e)

BLK = 128
LANES = 128
SUBLANES = 8
TOKEN_BLOCK = 1024
VMEM_LIMIT_BYTES = 56 * 1024 * 1024

C_GM_U, C_GM_V, C_GM_Z = 0, 256, 512
C_GLA_Q, C_GLA_K, C_GLA_V, C_GLA_Z = 768, 896, 1024, 1280
C_SWA_Q, C_SWA_K, C_SWA_V, C_SWA_Z = 1536, 2048, 2304, 2432
C_GLA_GLR = 2944
D_IN_PAD = 3072
Y_A, Y_B, Y_C = 0, 256, 512


def _swa_out_order():
    r = np.arange(HEAD_DIM)
    return np.concatenate(
        [np.concatenate([64 * c + r, 64 * (SWA_GROUP + c) + r]) for c in range(SWA_GROUP)])


def _in_proj_columns():
    o = np.cumsum((0, 256, 256, 256, 128, 128, 256, 16, 256, 512, 128, 128, 512))
    (o_u, o_v, o_z, o_q, o_k, o_gv, o_glr, o_gz, o_sq, o_sk, o_sv, o_sz, _) = o
    half = HEAD_DIM // 2
    r = np.arange
    cols = [o_u + r(256), o_v + r(256), o_z + r(256),
            o_q + r(128), o_k + r(128), o_gv + r(256), o_gz + r(256)]
    for p in range(SWA_Q_HEADS // 2):
        a, b = o_sq + 64 * (2 * p), o_sq + 64 * (2 * p + 1)
        cols += [a + r(half), b + r(half), a + half + r(half), b + half + r(half)]
    for g in range(SWA_KV_HEADS):
        a = o_sk + 64 * g
        cols += [a + r(half), a + r(half), a + half + r(half), a + half + r(half)]
    cols += [o_sv + r(128)]
    cols += [_swa_out_order() + o_sz]
    cols += [o_glr + r(GLA_GATE_RANK), np.full(LANES - GLA_GATE_RANK, -1)]
    cols = np.concatenate(cols)
    assert cols.shape == (D_IN_PAD,)
    return cols


def _runs(idx):
    idx = np.asarray(idx)
    runs, start = [], 0
    for end in range(1, len(idx) + 1):
        if end < len(idx) and idx[end] == idx[end - 1] + 1 and idx[end - 1] >= 0:
            continue
        if end < len(idx) and idx[end] < 0 and idx[end - 1] < 0:
            continue
        runs.append((start, int(idx[start]), end - start))
        start = end
    return runs


def _relayout_rows_kernel(w_ref, o_ref, *, runs):
    for dst, src, n in runs:
        o_ref[dst:dst + n, :] = w_ref[src:src + n, :].astype(o_ref.dtype)


def _relayout_rows(w, idx):
    depth, r, c = w.shape
    return pl.pallas_call(
        functools.partial(_relayout_rows_kernel, runs=_runs(idx)),
        grid=(depth,),
        in_specs=[pl.BlockSpec((None, r, c), lambda l: (l, 0, 0))],
        out_specs=pl.BlockSpec((None, len(idx), c), lambda l: (l, 0, 0)),
        out_shape=jax.ShapeDtypeStruct((depth, len(idx), c), BF16),
        compiler_params=pltpu.CompilerParams(vmem_limit_bytes=VMEM_LIMIT_BYTES),
        name="weight_relayout_rows",
    )(w)


def _dot_hi_lo(a, b):
    a_hi, b_hi = a.astype(BF16), b.astype(BF16)
    a_lo = (a - a_hi.astype(F32)).astype(BF16)
    b_lo = (b - b_hi.astype(F32)).astype(BF16)
    return _dot(a_hi, b_hi) + _dot(a_hi, b_lo) + _dot(a_lo, b_hi)


def _relayout_t_kernel(wt_ref, wg2t_ref, o_ref, *, idx, block):
    for j in range(len(idx) // block):
        pieces = []
        for _, src, n in _runs(idx[j * block:(j + 1) * block]):
            pieces.append(jnp.zeros((n, wt_ref.shape[1]), F32) if src < 0 else wt_ref[src:src + n, :])
        rows_t = jnp.concatenate(pieces, axis=0)
        lo = C_GLA_GLR - j * block
        if 0 <= lo < block:
            folded = _dot_hi_lo(wg2t_ref[...], rows_t[lo:lo + LANES])
            parts = [rows_t[:lo], folded, rows_t[lo + LANES:]]
            rows_t = jnp.concatenate([p for p in parts if p.shape[0]], axis=0)
        o_ref[:, j * block:(j + 1) * block] = rows_t.T.astype(o_ref.dtype)


def _relayout_t(wt, wg2t, idx, block):
    depth, c, r = wt.shape
    assert len(idx) % block == 0
    return pl.pallas_call(
        functools.partial(_relayout_t_kernel, idx=np.asarray(idx), block=block),
        grid=(depth,),
        in_specs=[pl.BlockSpec((None, c, r), lambda l: (l, 0, 0)),
                  pl.BlockSpec((None, LANES, LANES), lambda l: (l, 0, 0))],
        out_specs=pl.BlockSpec((None, r, len(idx)), lambda l: (l, 0, 0)),
        out_shape=jax.ShapeDtypeStruct((depth, r, len(idx)), BF16),
        compiler_params=pltpu.CompilerParams(vmem_limit_bytes=VMEM_LIMIT_BYTES),
        name="weight_relayout_t",
    )(wt, wg2t)


def _silu(z):
    hz = 0.5 * z
    return hz + hz * jnp.tanh(hz)


def _dot(a, b):
    return jnp.dot(a, b, preferred_element_type=F32)


def _dot_nt(a, b):
    return lax.dot_general(a, b, (((1,), (1,)), ((), ())), preferred_element_type=F32)


def _rope_table_kernel(pos_ref, invf_ref, cos_ref, sin_ref):
    half = HEAD_DIM // 2
    ang = pos_ref[...] * invf_ref[...]
    cos_d, sin_d = jnp.cos(ang), jnp.sin(ang)
    lane = lax.broadcasted_iota(jnp.int32, ang.shape, 1)

    def spread(t, m):
        if m:
            t = pltpu.roll(t, LANES - half * m, 1)
        t = jnp.where(lane < half, t, 0.0)
        t = t + pltpu.roll(t, half, 1)
        return t + pltpu.roll(t, 2 * half, 1)

    for m in range(LANES // half):
        cos_ref[m] = spread(cos_d, m)
        s = spread(sin_d, m)
        sin_ref[m] = jnp.where(lane < LANES // 2, -s, s)


def _layer_kernel(layer_ref, sinks_ref, x_ref, cos_ref, sin_ref, pre_g_ref, w_in_ref, lng_ref, lnb_ref,
                  ws_ref, bs_ref, bg_ref, gng_ref, w_out_ref, post_g_ref,
                  o_ref, proj_ref, y_ref, state_ref, kprev_ref, vprev_ref,
                  gcum_ref, gk_ref, gv_ref, states_ref, *, n_sub):
    step = pl.program_id(1)

    @pl.when(step == 0)
    def _():
        state_ref[...] = jnp.zeros_like(state_ref)
        kprev_ref[...] = jnp.zeros_like(kprev_ref)
        vprev_ref[...] = jnp.zeros_like(vprev_ref)

    x = x_ref[...]
    h = x * lax.rsqrt(jnp.mean(x * x, axis=-1, keepdims=True) + NORM_EPS) * pre_g_ref[...]
    proj_ref[...] = _dot(h.astype(BF16), w_in_ref[...])

    def sub_block(i, carry):
        rows = pl.ds(pl.multiple_of(i * BLK, BLK), BLK)
        row = lax.broadcasted_iota(jnp.int32, (BLK, BLK), 0)
        col = lax.broadcasted_iota(jnp.int32, (BLK, BLK), 1)
        lane = lax.broadcasted_iota(jnp.int32, (BLK, LANES), 1)
        n_chunk = BLK // GLA_CHUNK

        def seg(c0, width):
            return proj_ref[rows, c0:c0 + width]

        v = seg(C_GM_V, GM_WIDTH)
        mu = jnp.mean(v, axis=-1, keepdims=True)
        vc = v - mu
        var = jnp.mean(vc * vc, axis=-1, keepdims=True)

        logit = seg(C_GLA_GLR, LANES) + bg_ref[...]

        cosb = cos_ref[rows, :]
        sinb = sin_ref[rows, :]

        def rope(t):
            return t * cosb + pltpu.roll(t, LANES // 2, 1) * sinb

        lane2 = lax.broadcasted_iota(jnp.int32, (2 * BLK, LANES), 1)
        k_cur = [rope(seg(C_SWA_K + LANES * g, LANES)).astype(BF16) for g in range(SWA_KV_HEADS)]
        v_cur = seg(C_SWA_V, SWA_KV_WIDTH).astype(BF16)
        k_all = [jnp.concatenate([kprev_ref[g], k_cur[g]], axis=0)
                 for g in range(SWA_KV_HEADS)]
        v_all = jnp.concatenate([vprev_ref[...], v_cur], axis=0)
        one = jnp.ones((), BF16)
        v_aug = [jnp.where(lane2 < HEAD_DIM, v_all, one), jnp.where(lane2 < HEAD_DIM, one, v_all)]
        tq = lax.broadcasted_iota(jnp.int32, (BLK, 2 * BLK), 0)
        kc = lax.broadcasted_iota(jnp.int32, (BLK, 2 * BLK), 1)
        first_block = (step * n_sub + i) == 0
        prev_lo = jnp.where(first_block, BLK, 0)
        in_window = ((kc < BLK) & (kc > tq + prev_lo)) | ((kc >= BLK) & (kc - BLK <= tq))
        bias = jnp.where(in_window, 0.0, -jnp.inf)

        vln = (vc * lax.rsqrt(var + NORM_EPS) * lng_ref[...] + lnb_ref[...]).astype(BF16)
        w_tril = jnp.concatenate(
            [jnp.where(col <= row, ws_ref[hh], 0.0) for hh in range(GM_HEADS)], axis=0).astype(BF16)
        sv_all = _dot(w_tril, vln)
        sv_h = [sv_all[hh * BLK:(hh + 1) * BLK] for hh in range(GM_HEADS)]

        log_a = (jnp.minimum(logit, 0.0) - jnp.log1p(jnp.exp(-jnp.abs(logit)))) * (LOG2E / GLA_TAU)
        same_chunk = (row // GLA_CHUNK) == (col // GLA_CHUNK)
        tri16 = jnp.where(same_chunk & (col <= row), 1.0, 0.0).astype(BF16)
        ones16 = jnp.where(same_chunk, 1.0, 0.0).astype(BF16)
        la_hi = log_a.astype(BF16)
        la_lo = (log_a - la_hi.astype(F32)).astype(BF16)
        sums = _dot(jnp.concatenate([tri16, ones16], axis=0), jnp.concatenate([la_hi, la_lo], axis=1))
        g_cum = sums[:BLK, :LANES] + sums[:BLK, LANES:]
        g_end = sums[BLK:, :LANES] + sums[BLK:, LANES:]

        q_pairs = [rope(seg(C_SWA_Q + LANES * p, LANES)) * (HEAD_DIM ** -0.5 * LOG2E)
                   for p in range(SWA_Q_HEADS // 2)]
        lane_parity = (lane // (HEAD_DIM // 2)) % 2

        q = seg(C_GLA_Q, GLA_KEY_WIDTH) * (GLA_DK ** -0.5)
        k = seg(C_GLA_K, GLA_KEY_WIDTH)
        gv = seg(C_GLA_V, GLA_WIDTH)
        gcum_ref[...] = g_cum
        gk_ref[...] = k
        gv_ref[...] = gv
        hk = lax.broadcasted_iota(jnp.int32, (GLA_KEY_WIDTH, GLA_WIDTH), 0) // GLA_DK
        hv = lax.broadcasted_iota(jnp.int32, (GLA_KEY_WIDTH, GLA_WIDTH), 1) // GLA_DV
        head_match = hk == hv
        expand = jnp.where(head_match, 1.0, 0.0).astype(BF16)

        def state_updates():
            k_dec = k * jnp.exp2(g_end - g_cum)
            k_dec_t = k_dec.T
            lhs = jnp.concatenate(
                [jnp.where(col // GLA_CHUNK == jj, k_dec_t, 0.0) for jj in range(n_chunk)], axis=0)
            return _dot(lhs.astype(BF16), gv.astype(BF16))

        upd = state_updates()
        q_dec = q * jnp.exp2(g_cum)
        a_t = jnp.exp2(g_end).T

        t_local = row % GLA_CHUNK

        def chunk_rows(ref, s, n):
            return jnp.concatenate(
                [jnp.broadcast_to(ref[pl.ds(c * GLA_CHUNK + s, 1), :], (n, ref.shape[1]))
                 for c in range(n_chunk)], axis=0)

        def upper_half(t):
            return t.reshape(n_chunk, 2, SUBLANES, t.shape[1])[:, 1].reshape(BLK // 2, t.shape[1])

        q_up, g_up, t_up = upper_half(q), upper_half(g_cum), upper_half(t_local)

        def intra_terms(s_list):
            upper = s_list[0] >= SUBLANES
            qq, gg, tt, n = (q_up, g_up, t_up, SUBLANES) if upper else (q, g_cum, t_local, GLA_CHUNK)
            terms = []
            for s in s_list:
                term = qq * chunk_rows(gk_ref, s, n) * jnp.exp2(gg - chunk_rows(gcum_ref, s, n))
                terms.append(jnp.where(tt >= s, term, 0.0).astype(BF16))
            spread = _dot(jnp.concatenate(terms, axis=0), expand)
            m = qq.shape[0]
            out = spread[0:m] * chunk_rows(gv_ref, s_list[0], n)
            for i, s in enumerate(s_list[1:], 1):
                out = out + spread[i * m:(i + 1) * m] * chunk_rows(gv_ref, s, n)
            return out

        def attention_group(g):
            qm = []
            for hh in range(g * SWA_GROUP, (g + 1) * SWA_GROUP):
                p, par = divmod(hh, 2)
                qm.append(jnp.where(lane_parity == par, q_pairs[p], 0.0).astype(BF16))
            sc = _dot_nt(jnp.concatenate(qm, axis=0), k_all[g])
            es, sinks_w = [], []
            for i, hh in enumerate(range(g * SWA_GROUP, (g + 1) * SWA_GROUP)):
                sc_h = sc[i * BLK:(i + 1) * BLK] + bias
                sink = sinks_ref[layer_ref[0], hh] * LOG2E
                m = jnp.maximum(jnp.max(sc_h, axis=-1, keepdims=True), sink)
                es.append(jnp.exp2(sc_h - m).astype(BF16))
                sinks_w.append(jnp.exp2(sink - m))
            pv_all = _dot(jnp.concatenate(es, axis=0), v_aug[g])
            return [pv_all[i * BLK:(i + 1) * BLK] for i in range(SWA_GROUP)], sinks_w

        half = SUBLANES // 2
        o_lo = intra_terms(list(range(0, half)))
        pv, sink_w = attention_group(0)

        state = state_ref[...]
        for jj in range(n_chunk):
            states_ref[jj * BLK:(jj + 1) * BLK, :] = state.astype(BF16)
            a_j = a_t[:, jj * GLA_CHUNK:jj * GLA_CHUNK + 1]
            state = a_j * state + jnp.where(head_match, upd[jj * BLK:(jj + 1) * BLK], 0.0)
        state_ref[...] = state
        q_blk = jnp.concatenate(
            [jnp.where(row // GLA_CHUNK == jj, q_dec, 0.0) for jj in range(n_chunk)], axis=1)
        o_inter = _dot(q_blk.astype(BF16), states_ref[...])

        o_lo = o_lo + intra_terms(list(range(half, SUBLANES)))
        o_up = intra_terms(list(range(SUBLANES, SUBLANES + half)))
        pv1, sink_w1 = attention_group(1)
        pv, sink_w = pv + pv1, sink_w + sink_w1
        o_up = o_up + intra_terms(list(range(SUBLANES + half, GLA_CHUNK)))

        o_up = jnp.concatenate(
            [jnp.zeros((n_chunk, 1, SUBLANES, GLA_WIDTH), F32),
             o_up.reshape(n_chunk, 1, SUBLANES, GLA_WIDTH)], axis=1).reshape(BLK, GLA_WIDTH)
        o_gla = o_lo + o_up + o_inter
        seg_r = lax.broadcasted_iota(jnp.int32, (GLA_WIDTH, GLA_WIDTH), 0) // GLA_DV
        seg_c = lax.broadcasted_iota(jnp.int32, (GLA_WIDTH, GLA_WIDTH), 1) // GLA_DV
        head_ones = jnp.where(seg_r == seg_c, 1.0, 0.0).astype(BF16)
        sq = o_gla * o_gla
        sq_hi = sq.astype(BF16)
        sq_lo = (sq - sq_hi.astype(F32)).astype(BF16)
        sums = _dot(jnp.concatenate([sq_hi, sq_lo], axis=0), head_ones)
        ms = (sums[:BLK] + sums[BLK:]) * (1.0 / GLA_DV)

        for c in range(SWA_GROUP):
            lo_head = lane < HEAD_DIM
            num = jnp.where(lo_head, pv[c], pv[SWA_GROUP + c])
            den = pltpu.roll(jnp.where(lo_head, pv[SWA_GROUP + c], pv[c]), LANES // 2, 1)
            den = den + jnp.where(lo_head, sink_w[c], sink_w[SWA_GROUP + c])
            yc = num / den * _silu(seg(C_SWA_Z + LANES * c, LANES))
            y_ref[rows, Y_C + LANES * c:Y_C + LANES * (c + 1)] = yc.astype(BF16)
        for g in range(SWA_KV_HEADS):
            kprev_ref[g] = k_cur[g]
        vprev_ref[...] = v_cur

        yb = o_gla * lax.rsqrt(ms + NORM_EPS) * gng_ref[...]
        yb = yb * _silu(seg(C_GLA_Z, GLA_WIDTH))
        y_ref[rows, Y_B:Y_B + GLA_WIDTH] = yb.astype(BF16)
        lane_head = lax.broadcasted_iota(jnp.int32, (BLK, GM_WIDTH), 1) // HEAD_DIM
        sv = sv_h[0]
        for hh in range(1, GM_HEADS):
            sv = jnp.where(lane_head == hh, sv_h[hh], sv)
        ya = seg(C_GM_U, GM_WIDTH) * (sv + bs_ref[...]) * _silu(seg(C_GM_Z, GM_WIDTH))
        y_ref[rows, Y_A:Y_A + GM_WIDTH] = ya.astype(BF16)
        return carry

    lax.fori_loop(0, n_sub, sub_block, 0)

    y2 = _dot(y_ref[...], w_out_ref[...])
    y2 = y2 * lax.rsqrt(jnp.mean(y2 * y2, axis=-1, keepdims=True) + NORM_EPS) * post_g_ref[...]
    o_ref[...] = x_ref[...] + y2


def _rope_tables(positions):
    n = positions.size
    half = HEAD_DIM // 2
    inv_freq = jnp.power(ROPE_THETA, -jnp.arange(half, dtype=F32) * 2.0 / HEAD_DIM)
    per_row = LANES // half
    quarter = n // per_row
    invf = jnp.tile(inv_freq, per_row).reshape(1, LANES)
    pos = jnp.repeat(positions.reshape(per_row, quarter).T.astype(F32), half, axis=1)
    rows = 1024
    assert quarter % rows == 0
    out_spec = pl.BlockSpec((per_row, rows, LANES), lambda i: (0, i, 0))
    cos_t, sin_t = pl.pallas_call(
        _rope_table_kernel,
        grid=(quarter // rows,),
        in_specs=[pl.BlockSpec((rows, LANES), lambda i: (i, 0)), pl.BlockSpec((1, LANES), lambda i: (0, 0))],
        out_specs=[out_spec, out_spec],
        out_shape=[jax.ShapeDtypeStruct((per_row, quarter, LANES), F32)] * 2,
        name="rope_tables",
    )(pos, invf)
    return cos_t.reshape(n, LANES), sin_t.reshape(n, LANES)


def _layer_call(batch, seq):
    tb = TOKEN_BLOCK
    n_sub = tb // BLK
    steps = seq // tb
    tok = lambda b, j, l: (b * steps + j, 0)
    layer3 = lambda b, j, l: (l[0], 0, 0)
    layer4 = lambda b, j, l: (l[0], 0, 0, 0)
    in_specs = [
        pl.BlockSpec(memory_space=pltpu.SMEM),
        pl.BlockSpec((tb, D_MODEL), tok),
        pl.BlockSpec((tb, LANES), tok),
        pl.BlockSpec((tb, LANES), tok),
        pl.BlockSpec((None, 1, D_MODEL), layer3),
        pl.BlockSpec((None, D_MODEL, D_IN_PAD), layer3),
        pl.BlockSpec((None, 1, GM_WIDTH), layer3),
        pl.BlockSpec((None, 1, GM_WIDTH), layer3),
        pl.BlockSpec((None, GM_HEADS, BLK, BLK), layer4),
        pl.BlockSpec((None, BLK, GM_WIDTH), layer3),
        pl.BlockSpec((None, 1, GLA_KEY_WIDTH), layer3),
        pl.BlockSpec((None, 1, GLA_WIDTH), layer3),
        pl.BlockSpec((None, D_MIX, D_MODEL), layer3),
        pl.BlockSpec((None, 1, D_MODEL), layer3),
    ]
    return pl.pallas_call(
        functools.partial(_layer_kernel, n_sub=n_sub),
        grid_spec=pltpu.PrefetchScalarGridSpec(
            num_scalar_prefetch=1,
            grid=(batch, steps),
            in_specs=in_specs,
            out_specs=pl.BlockSpec((tb, D_MODEL), tok),
            scratch_shapes=[
                pltpu.VMEM((tb, D_IN_PAD), F32),
                pltpu.VMEM((tb, D_MIX), BF16),
                pltpu.VMEM((GLA_KEY_WIDTH, GLA_WIDTH), F32),
                pltpu.VMEM((SWA_KV_HEADS, BLK, LANES), BF16),
                pltpu.VMEM((BLK, SWA_KV_WIDTH), BF16),
                pltpu.VMEM((BLK, GLA_KEY_WIDTH), F32),
                pltpu.VMEM((BLK, GLA_KEY_WIDTH), F32),
                pltpu.VMEM((BLK, GLA_WIDTH), F32),
                pltpu.VMEM((BLK // GLA_CHUNK * GLA_KEY_WIDTH, GLA_WIDTH), BF16),
            ]),
        out_shape=jax.ShapeDtypeStruct((batch * seq, D_MODEL), F32),
        compiler_params=pltpu.CompilerParams(
            dimension_semantics=("arbitrary", "arbitrary"),
            vmem_limit_bytes=VMEM_LIMIT_BYTES),
        name="hybrid_layer",
    )


def kernel(x, positions, pre_g, w_in, gm_ln_g, gm_ln_b, gm_ws, gm_bs, gla_wg2, gla_bg, gla_norm_g,
           swa_sinks, w_out, post_g):
    batch, seq, d_model = x.shape
    assert d_model == D_MODEL and seq % TOKEN_BLOCK == 0
    assert w_in.shape == (DEPTH, D_MODEL, D_IN) and w_out.shape == (DEPTH, D_MIX, D_MODEL)

    row = lambda p: p[:, None, :]
    wg2t = jnp.pad(jnp.swapaxes(gla_wg2, 1, 2), ((0, 0), (0, 0), (0, LANES - GLA_GATE_RANK)))
    w_in_p = _relayout_t(jnp.swapaxes(w_in, 1, 2), wg2t, _in_proj_columns(), block=2 * LANES)
    out_rows = np.concatenate([np.arange(Y_C), Y_C + _swa_out_order()])
    w_out_p = _relayout_rows(w_out, out_rows)
    bs_p = jnp.repeat(jnp.swapaxes(gm_bs, 1, 2), HEAD_DIM, axis=2)
    gng_p = jnp.tile(gla_norm_g, (1, GLA_HEADS))

    cos_t, sin_t = _rope_tables(positions)
    layer = _layer_call(batch, seq)
    h = x.reshape(batch * seq, D_MODEL)
    for l in range(DEPTH):
        h = layer(jnp.full((1,), l, jnp.int32), swa_sinks, h, cos_t, sin_t, row(pre_g), w_in_p,
                  row(gm_ln_g), row(gm_ln_b), gm_ws, bs_p, row(gla_bg), row(gng_p),
                  w_out_p, row(post_g))
    return h.reshape(batch, seq, D_MODEL)
```

```python
import functools
import math

import numpy as np
import jax
import jax.numpy as jnp
from jax import lax
from jax.experimental import pallas as pl
from jax.experimental.pallas import tpu as pltpu

F32 = jnp.float32
BF16 = jnp.bfloat16

D_MODEL = 1024
DEPTH = 4
HEAD_DIM = 64
NORM_EPS = 1e-6
GM_HEADS = 4
GM_WIDTH = 256
GLA_HEADS = 4
GLA_DV = 64
GLA_DK = 32
GLA_WIDTH = 256
GLA_KEY_WIDTH = 128
GLA_GATE_RANK = 16
GLA_TAU = 16.0
GLA_CHUNK = 16
SWA_Q_HEADS = 8
SWA_KV_HEADS = 2
SWA_GROUP = SWA_Q_HEADS // SWA_KV_HEADS
SWA_WIDTH = 512
SWA_KV_WIDTH = 128
ROPE_THETA = 10000.0
D_MIX = 1024
D_IN = 2832
LOG2E = math.log2(math.e)

BLK = 128
LANES = 128
SUBLANES = 8
TOKEN_BLOCK = 1024
VMEM_LIMIT_BYTES = 56 * 1024 * 1024
MIXER_STAGE_ORDER = ("L0", "G0", "L1", "S", "U0", "G1", "U1")

C_GM_U, C_GM_V, C_GM_Z = 0, 256, 512
C_GLA_Q, C_GLA_K, C_GLA_V, C_GLA_Z = 768, 896, 1024, 1280
C_SWA_Q, C_SWA_K, C_SWA_V, C_SWA_Z = 1536, 2048, 2304, 2432
C_GLA_GLR = 2944
D_IN_PAD = 3072
Y_A, Y_B, Y_C = 0, 256, 512


def _swa_out_order():
    r = np.arange(HEAD_DIM)
    return np.concatenate(
        [np.concatenate([64 * c + r, 64 * (SWA_GROUP + c) + r]) for c in range(SWA_GROUP)])


def _in_proj_columns():
    o = np.cumsum((0, 256, 256, 256, 128, 128, 256, 16, 256, 512, 128, 128, 512))
    (o_u, o_v, o_z, o_q, o_k, o_gv, o_glr, o_gz, o_sq, o_sk, o_sv, o_sz, _) = o
    half = HEAD_DIM // 2
    r = np.arange
    cols = [o_u + r(256), o_v + r(256), o_z + r(256),
            o_q + r(128), o_k + r(128), o_gv + r(256), o_gz + r(256)]
    for p in range(SWA_Q_HEADS // 2):
        a, b = o_sq + 64 * (2 * p), o_sq + 64 * (2 * p + 1)
        cols += [a + r(half), b + r(half), a + half + r(half), b + half + r(half)]
    for g in range(SWA_KV_HEADS):
        a = o_sk + 64 * g
        cols += [a + r(half), a + r(half), a + half + r(half), a + half + r(half)]
    cols += [o_sv + r(128)]
    cols += [_swa_out_order() + o_sz]
    cols += [o_glr + r(GLA_GATE_RANK), np.full(LANES - GLA_GATE_RANK, -1)]
    cols = np.concatenate(cols)
    assert cols.shape == (D_IN_PAD,)
    return cols


def _runs(idx):
    idx = np.asarray(idx)
    runs, start = [], 0
    for end in range(1, len(idx) + 1):
        if end < len(idx) and idx[end] == idx[end - 1] + 1 and idx[end - 1] >= 0:
            continue
        if end < len(idx) and idx[end] < 0 and idx[end - 1] < 0:
            continue
        runs.append((start, int(idx[start]), end - start))
        start = end
    return runs


def _relayout_rows_kernel(w_ref, o_ref, *, runs):
    for dst, src, n in runs:
        o_ref[dst:dst + n, :] = w_ref[src:src + n, :].astype(o_ref.dtype)


def _relayout_rows(w, idx):
    depth, r, c = w.shape
    return pl.pallas_call(
        functools.partial(_relayout_rows_kernel, runs=_runs(idx)),
        grid=(depth,),
        in_specs=[pl.BlockSpec((None, r, c), lambda l: (l, 0, 0))],
        out_specs=pl.BlockSpec((None, len(idx), c), lambda l: (l, 0, 0)),
        out_shape=jax.ShapeDtypeStruct((depth, len(idx), c), BF16),
        compiler_params=pltpu.CompilerParams(vmem_limit_bytes=VMEM_LIMIT_BYTES),
        name="weight_relayout_rows",
    )(w)


def _dot_hi_lo(a, b):
    a_hi, b_hi = a.astype(BF16), b.astype(BF16)
    a_lo = (a - a_hi.astype(F32)).astype(BF16)
    b_lo = (b - b_hi.astype(F32)).astype(BF16)
    return _dot(a_hi, b_hi) + _dot(a_hi, b_lo) + _dot(a_lo, b_hi)


def _relayout_t_kernel(wt_ref, wg2t_ref, o_ref, *, idx, block):
    for j in range(len(idx) // block):
        pieces = []
        for _, src, n in _runs(idx[j * block:(j + 1) * block]):
            pieces.append(jnp.zeros((n, wt_ref.shape[1]), F32) if src < 0 else wt_ref[src:src + n, :])
        rows_t = jnp.concatenate(pieces, axis=0)
        lo = C_GLA_GLR - j * block
        if 0 <= lo < block:
            folded = _dot_hi_lo(wg2t_ref[...], rows_t[lo:lo + LANES])
            parts = [rows_t[:lo], folded, rows_t[lo + LANES:]]
            rows_t = jnp.concatenate([p for p in parts if p.shape[0]], axis=0)
        o_ref[:, j * block:(j + 1) * block] = rows_t.T.astype(o_ref.dtype)


def _relayout_t(wt, wg2t, idx, block):
    depth, c, r = wt.shape
    assert len(idx) % block == 0
    return pl.pallas_call(
        functools.partial(_relayout_t_kernel, idx=np.asarray(idx), block=block),
        grid=(depth,),
        in_specs=[pl.BlockSpec((None, c, r), lambda l: (l, 0, 0)),
                  pl.BlockSpec((None, LANES, LANES), lambda l: (l, 0, 0))],
        out_specs=pl.BlockSpec((None, r, len(idx)), lambda l: (l, 0, 0)),
        out_shape=jax.ShapeDtypeStruct((depth, r, len(idx)), BF16),
        compiler_params=pltpu.CompilerParams(vmem_limit_bytes=VMEM_LIMIT_BYTES),
        name="weight_relayout_t",
    )(wt, wg2t)


def _silu(z):
    hz = 0.5 * z
    return hz + hz * jnp.tanh(hz)


def _dot(a, b):
    return jnp.dot(a, b, preferred_element_type=F32)


def _dot_nt(a, b):
    return lax.dot_general(a, b, (((1,), (1,)), ((), ())), preferred_element_type=F32)


def _rope_table_kernel(pos_ref, invf_ref, cos_ref, sin_ref):
    half = HEAD_DIM // 2
    ang = pos_ref[...] * invf_ref[...]
    cos_d, sin_d = jnp.cos(ang), jnp.sin(ang)
    lane = lax.broadcasted_iota(jnp.int32, ang.shape, 1)

    def spread(t, m):
        if m:
            t = pltpu.roll(t, LANES - half * m, 1)
        t = jnp.where(lane < half, t, 0.0)
        t = t + pltpu.roll(t, half, 1)
        return t + pltpu.roll(t, 2 * half, 1)

    for m in range(LANES // half):
        cos_ref[m] = spread(cos_d, m)
        s = spread(sin_d, m)
        sin_ref[m] = jnp.where(lane < LANES // 2, -s, s)


def _layer_kernel(layer_ref, sinks_ref, x_ref, cos_ref, sin_ref, pre_g_ref, w_in_ref, lng_ref, lnb_ref,
                  ws_ref, bs_ref, bg_ref, gng_ref, w_out_ref, post_g_ref,
                  o_ref, proj_ref, y_ref, state_ref, kprev_ref, vprev_ref,
                  gcum_ref, gk_ref, gv_ref, states_ref, *, n_sub):
    step = pl.program_id(1)

    @pl.when(step == 0)
    def _():
        state_ref[...] = jnp.zeros_like(state_ref)
        kprev_ref[...] = jnp.zeros_like(kprev_ref)
        vprev_ref[...] = jnp.zeros_like(vprev_ref)

    x = x_ref[...]
    h = x * lax.rsqrt(jnp.mean(x * x, axis=-1, keepdims=True) + NORM_EPS) * pre_g_ref[...]
    proj_ref[...] = _dot(h.astype(BF16), w_in_ref[...])

    def sub_block(i, carry):
        rows = pl.ds(pl.multiple_of(i * BLK, BLK), BLK)
        row = lax.broadcasted_iota(jnp.int32, (BLK, BLK), 0)
        col = lax.broadcasted_iota(jnp.int32, (BLK, BLK), 1)
        lane = lax.broadcasted_iota(jnp.int32, (BLK, LANES), 1)
        n_chunk = BLK // GLA_CHUNK

        def seg(c0, width):
            return proj_ref[rows, c0:c0 + width]

        v = seg(C_GM_V, GM_WIDTH)
        mu = jnp.mean(v, axis=-1, keepdims=True)
        vc = v - mu
        var = jnp.mean(vc * vc, axis=-1, keepdims=True)

        logit = seg(C_GLA_GLR, LANES) + bg_ref[...]

        cosb = cos_ref[rows, :]
        sinb = sin_ref[rows, :]

        def rope(t):
            return t * cosb + pltpu.roll(t, LANES // 2, 1) * sinb

        lane2 = lax.broadcasted_iota(jnp.int32, (2 * BLK, LANES), 1)
        k_cur = [rope(seg(C_SWA_K + LANES * g, LANES)).astype(BF16) for g in range(SWA_KV_HEADS)]
        v_cur = seg(C_SWA_V, SWA_KV_WIDTH).astype(BF16)
        k_all = [jnp.concatenate([kprev_ref[g], k_cur[g]], axis=0)
                 for g in range(SWA_KV_HEADS)]
        v_all = jnp.concatenate([vprev_ref[...], v_cur], axis=0)
        one = jnp.ones((), BF16)
        v_aug = [jnp.where(lane2 < HEAD_DIM, v_all, one), jnp.where(lane2 < HEAD_DIM, one, v_all)]
        tq = lax.broadcasted_iota(jnp.int32, (BLK, 2 * BLK), 0)
        kc = lax.broadcasted_iota(jnp.int32, (BLK, 2 * BLK), 1)
        first_block = (step * n_sub + i) == 0
        prev_lo = jnp.where(first_block, BLK, 0)
        in_window = ((kc < BLK) & (kc > tq + prev_lo)) | ((kc >= BLK) & (kc - BLK <= tq))
        bias = jnp.where(in_window, 0.0, -jnp.inf)

        vln = (vc * lax.rsqrt(var + NORM_EPS) * lng_ref[...] + lnb_ref[...]).astype(BF16)
        w_tril = jnp.concatenate(
            [jnp.where(col <= row, ws_ref[hh], 0.0) for hh in range(GM_HEADS)], axis=0).astype(BF16)
        sv_all = _dot(w_tril, vln)
        sv_h = [sv_all[hh * BLK:(hh + 1) * BLK] for hh in range(GM_HEADS)]

        log_a = (jnp.minimum(logit, 0.0) - jnp.log1p(jnp.exp(-jnp.abs(logit)))) * (LOG2E / GLA_TAU)
        same_chunk = (row // GLA_CHUNK) == (col // GLA_CHUNK)
        tri16 = jnp.where(same_chunk & (col <= row), 1.0, 0.0).astype(BF16)
        ones16 = jnp.where(same_chunk, 1.0, 0.0).astype(BF16)
        la_hi = log_a.astype(BF16)
        la_lo = (log_a - la_hi.astype(F32)).astype(BF16)
        sums = _dot(jnp.concatenate([tri16, ones16], axis=0), jnp.concatenate([la_hi, la_lo], axis=1))
        g_cum = sums[:BLK, :LANES] + sums[:BLK, LANES:]
        g_end = sums[BLK:, :LANES] + sums[BLK:, LANES:]

        q_pairs = [rope(seg(C_SWA_Q + LANES * p, LANES)) * (HEAD_DIM ** -0.5 * LOG2E)
                   for p in range(SWA_Q_HEADS // 2)]
        lane_parity = (lane // (HEAD_DIM // 2)) % 2

        q = seg(C_GLA_Q, GLA_KEY_WIDTH) * (GLA_DK ** -0.5)
        k = seg(C_GLA_K, GLA_KEY_WIDTH)
        gv = seg(C_GLA_V, GLA_WIDTH)
        gcum_ref[...] = g_cum
        gk_ref[...] = k
        gv_ref[...] = gv
        hk = lax.broadcasted_iota(jnp.int32, (GLA_KEY_WIDTH, GLA_WIDTH), 0) // GLA_DK
        hv = lax.broadcasted_iota(jnp.int32, (GLA_KEY_WIDTH, GLA_WIDTH), 1) // GLA_DV
        head_match = hk == hv
        expand = jnp.where(head_match, 1.0, 0.0).astype(BF16)

        def state_updates():
            k_dec = k * jnp.exp2(g_end - g_cum)
            k_dec_t = k_dec.T
            lhs = jnp.concatenate(
                [jnp.where(col // GLA_CHUNK == jj, k_dec_t, 0.0) for jj in range(n_chunk)], axis=0)
            return _dot(lhs.astype(BF16), gv.astype(BF16))

        upd = state_updates()
        q_dec = q * jnp.exp2(g_cum)
        a_t = jnp.exp2(g_end).T

        t_local = row % GLA_CHUNK

        def chunk_rows(ref, s, n):
            return jnp.concatenate(
                [jnp.broadcast_to(ref[pl.ds(c * GLA_CHUNK + s, 1), :], (n, ref.shape[1]))
                 for c in range(n_chunk)], axis=0)

        def upper_half(t):
            return t.reshape(n_chunk, 2, SUBLANES, t.shape[1])[:, 1].reshape(BLK // 2, t.shape[1])

        q_up, g_up, t_up = upper_half(q), upper_half(g_cum), upper_half(t_local)

        def intra_terms(s_list):
            upper = s_list[0] >= SUBLANES
            qq, gg, tt, n = (q_up, g_up, t_up, SUBLANES) if upper else (q, g_cum, t_local, GLA_CHUNK)
            terms = []
            for s in s_list:
                term = qq * chunk_rows(gk_ref, s, n) * jnp.exp2(gg - chunk_rows(gcum_ref, s, n))
                terms.append(jnp.where(tt >= s, term, 0.0).astype(BF16))
            spread = _dot(jnp.concatenate(terms, axis=0), expand)
            m = qq.shape[0]
            out = spread[0:m] * chunk_rows(gv_ref, s_list[0], n)
            for i, s in enumerate(s_list[1:], 1):
                out = out + spread[i * m:(i + 1) * m] * chunk_rows(gv_ref, s, n)
            return out

        def attention_group(g):
            qm = []
            for hh in range(g * SWA_GROUP, (g + 1) * SWA_GROUP):
                p, par = divmod(hh, 2)
                qm.append(jnp.where(lane_parity == par, q_pairs[p], 0.0).astype(BF16))
            sc = _dot_nt(jnp.concatenate(qm, axis=0), k_all[g])
            es, sinks_w = [], []
            for i, hh in enumerate(range(g * SWA_GROUP, (g + 1) * SWA_GROUP)):
                sc_h = sc[i * BLK:(i + 1) * BLK] + bias
                sink = sinks_ref[layer_ref[0], hh] * LOG2E
                m = jnp.maximum(jnp.max(sc_h, axis=-1, keepdims=True), sink)
                es.append(jnp.exp2(sc_h - m).astype(BF16))
                sinks_w.append(jnp.exp2(sink - m))
            pv_all = _dot(jnp.concatenate(es, axis=0), v_aug[g])
            return [pv_all[i * BLK:(i + 1) * BLK] for i in range(SWA_GROUP)], sinks_w

        def state_pass():
            state = state_ref[...]
            for jj in range(n_chunk):
                states_ref[jj * BLK:(jj + 1) * BLK, :] = state.astype(BF16)
                a_j = a_t[:, jj * GLA_CHUNK:jj * GLA_CHUNK + 1]
                state = a_j * state + jnp.where(head_match, upd[jj * BLK:(jj + 1) * BLK], 0.0)
            state_ref[...] = state
            q_blk = jnp.concatenate(
                [jnp.where(row // GLA_CHUNK == jj, q_dec, 0.0) for jj in range(n_chunk)], axis=1)
            return _dot(q_blk.astype(BF16), states_ref[...])

        half = SUBLANES // 2
        key_rows = {"L0": range(0, half), "L1": range(half, SUBLANES),
                    "U0": range(SUBLANES, SUBLANES + half), "U1": range(SUBLANES + half, GLA_CHUNK)}
        o_lo = o_up = o_inter = None
        pv, sink_w = [None] * SWA_Q_HEADS, [None] * SWA_Q_HEADS
        for stage in MIXER_STAGE_ORDER:
            if stage[0] == "L":
                term = intra_terms(list(key_rows[stage]))
                o_lo = term if o_lo is None else o_lo + term
            elif stage[0] == "U":
                term = intra_terms(list(key_rows[stage]))
                o_up = term if o_up is None else o_up + term
            elif stage[0] == "G":
                g = int(stage[1])
                pv[g * SWA_GROUP:(g + 1) * SWA_GROUP], sink_w[g * SWA_GROUP:(g + 1) * SWA_GROUP] = attention_group(g)
            else:
                o_inter = state_pass()

        o_up = jnp.concatenate(
            [jnp.zeros((n_chunk, 1, SUBLANES, GLA_WIDTH), F32),
             o_up.reshape(n_chunk, 1, SUBLANES, GLA_WIDTH)], axis=1).reshape(BLK, GLA_WIDTH)
        o_gla = o_lo + o_up + o_inter
        seg_r = lax.broadcasted_iota(jnp.int32, (GLA_WIDTH, GLA_WIDTH), 0) // GLA_DV
        seg_c = lax.broadcasted_iota(jnp.int32, (GLA_WIDTH, GLA_WIDTH), 1) // GLA_DV
        head_ones = jnp.where(seg_r == seg_c, 1.0, 0.0).astype(BF16)
        sq = o_gla * o_gla
        sq_hi = sq.astype(BF16)
        sq_lo = (sq - sq_hi.astype(F32)).astype(BF16)
        sums = _dot(jnp.concatenate([sq_hi, sq_lo], axis=0), head_ones)
        ms = (sums[:BLK] + sums[BLK:]) * (1.0 / GLA_DV)

        for c in range(SWA_GROUP):
            lo_head = lane < HEAD_DIM
            num = jnp.where(lo_head, pv[c], pv[SWA_GROUP + c])
            den = pltpu.roll(jnp.where(lo_head, pv[SWA_GROUP + c], pv[c]), LANES // 2, 1)
            den = den + jnp.where(lo_head, sink_w[c], sink_w[SWA_GROUP + c])
            yc = num / den * _silu(seg(C_SWA_Z + LANES * c, LANES))
            y_ref[rows, Y_C + LANES * c:Y_C + LANES * (c + 1)] = yc.astype(BF16)
        for g in range(SWA_KV_HEADS):
            kprev_ref[g] = k_cur[g]
        vprev_ref[...] = v_cur

        yb = o_gla * lax.rsqrt(ms + NORM_EPS) * gng_ref[...]
        yb = yb * _silu(seg(C_GLA_Z, GLA_WIDTH))
        y_ref[rows, Y_B:Y_B + GLA_WIDTH] = yb.astype(BF16)
        lane_head = lax.broadcasted_iota(jnp.int32, (BLK, GM_WIDTH), 1) // HEAD_DIM
        sv = sv_h[0]
        for hh in range(1, GM_HEADS):
            sv = jnp.where(lane_head == hh, sv_h[hh], sv)
        ya = seg(C_GM_U, GM_WIDTH) * (sv + bs_ref[...]) * _silu(seg(C_GM_Z, GM_WIDTH))
        y_ref[rows, Y_A:Y_A + GM_WIDTH] = ya.astype(BF16)
        return carry

    lax.fori_loop(0, n_sub, sub_block, 0)

    y2 = _dot(y_ref[...], w_out_ref[...])
    y2 = y2 * lax.rsqrt(jnp.mean(y2 * y2, axis=-1, keepdims=True) + NORM_EPS) * post_g_ref[...]
    o_ref[...] = x_ref[...] + y2


def _rope_tables(positions):
    n = positions.size
    half = HEAD_DIM // 2
    inv_freq = jnp.power(ROPE_THETA, -jnp.arange(half, dtype=F32) * 2.0 / HEAD_DIM)
    per_row = LANES // half
    quarter = n // per_row
    invf = jnp.tile(inv_freq, per_row).reshape(1, LANES)
    pos = jnp.repeat(positions.reshape(per_row, quarter).T.astype(F32), half, axis=1)
    rows = 1024
    assert quarter % rows == 0
    out_spec = pl.BlockSpec((per_row, rows, LANES), lambda i: (0, i, 0))
    cos_t, sin_t = pl.pallas_call(
        _rope_table_kernel,
        grid=(quarter // rows,),
        in_specs=[pl.BlockSpec((rows, LANES), lambda i: (i, 0)), pl.BlockSpec((1, LANES), lambda i: (0, 0))],
        out_specs=[out_spec, out_spec],
        out_shape=[jax.ShapeDtypeStruct((per_row, quarter, LANES), F32)] * 2,
        name="rope_tables",
    )(pos, invf)
    return cos_t.reshape(n, LANES), sin_t.reshape(n, LANES)


def _layer_call(batch, seq):
    tb = TOKEN_BLOCK
    n_sub = tb // BLK
    steps = seq // tb
    tok = lambda b, j, l: (b * steps + j, 0)
    layer3 = lambda b, j, l: (l[0], 0, 0)
    layer4 = lambda b, j, l: (l[0], 0, 0, 0)
    in_specs = [
        pl.BlockSpec(memory_space=pltpu.SMEM),
        pl.BlockSpec((tb, D_MODEL), tok),
        pl.BlockSpec((tb, LANES), tok),
        pl.BlockSpec((tb, LANES), tok),
        pl.BlockSpec((None, 1, D_MODEL), layer3),
        pl.BlockSpec((None, D_MODEL, D_IN_PAD), layer3),
        pl.BlockSpec((None, 1, GM_WIDTH), layer3),
        pl.BlockSpec((None, 1, GM_WIDTH), layer3),
        pl.BlockSpec((None, GM_HEADS, BLK, BLK), layer4),
        pl.BlockSpec((None, BLK, GM_WIDTH), layer3),
        pl.BlockSpec((None, 1, GLA_KEY_WIDTH), layer3),
        pl.BlockSpec((None, 1, GLA_WIDTH), layer3),
        pl.BlockSpec((None, D_MIX, D_MODEL), layer3),
        pl.BlockSpec((None, 1, D_MODEL), layer3),
    ]
    return pl.pallas_call(
        functools.partial(_layer_kernel, n_sub=n_sub),
        grid_spec=pltpu.PrefetchScalarGridSpec(
            num_scalar_prefetch=1,
            grid=(batch, steps),
            in_specs=in_specs,
            out_specs=pl.BlockSpec((tb, D_MODEL), tok),
            scratch_shapes=[
                pltpu.VMEM((tb, D_IN_PAD), F32),
                pltpu.VMEM((tb, D_MIX), BF16),
                pltpu.VMEM((GLA_KEY_WIDTH, GLA_WIDTH), F32),
                pltpu.VMEM((SWA_KV_HEADS, BLK, LANES), BF16),
                pltpu.VMEM((BLK, SWA_KV_WIDTH), BF16),
                pltpu.VMEM((BLK, GLA_KEY_WIDTH), F32),
                pltpu.VMEM((BLK, GLA_KEY_WIDTH), F32),
                pltpu.VMEM((BLK, GLA_WIDTH), F32),
                pltpu.VMEM((BLK // GLA_CHUNK * GLA_KEY_WIDTH, GLA_WIDTH), BF16),
            ]),
        out_shape=jax.ShapeDtypeStruct((batch * seq, D_MODEL), F32),
        compiler_params=pltpu.CompilerParams(
            dimension_semantics=("arbitrary", "arbitrary"),
            vmem_limit_bytes=VMEM_LIMIT_BYTES),
        name="hybrid_layer",
    )


def kernel(x, positions, pre_g, w_in, gm_ln_g, gm_ln_b, gm_ws, gm_bs, gla_wg2, gla_bg, gla_norm_g,
           swa_sinks, w_out, post_g):
    batch, seq, d_model = x.shape
    assert d_model == D_MODEL and seq % TOKEN_BLOCK == 0
    assert w_in.shape == (DEPTH, D_MODEL, D_IN) and w_out.shape == (DEPTH, D_MIX, D_MODEL)

    row = lambda p: p[:, None, :]
    wg2t = jnp.pad(jnp.swapaxes(gla_wg2, 1, 2), ((0, 0), (0, 0), (0, LANES - GLA_GATE_RANK)))
    w_in_p = _relayout_t(jnp.swapaxes(w_in, 1, 2), wg2t, _in_proj_columns(), block=2 * LANES)
    out_rows = np.concatenate([np.arange(Y_C), Y_C + _swa_out_order()])
    w_out_p = _relayout_rows(w_out, out_rows)
    bs_p = jnp.repeat(jnp.swapaxes(gm_bs, 1, 2), HEAD_DIM, axis=2)
    gng_p = jnp.tile(gla_norm_g, (1, GLA_HEADS))

    cos_t, sin_t = _rope_tables(positions)
    layer = _layer_call(batch, seq)
    h = x.reshape(batch * seq, D_MODEL)
    for l in range(DEPTH):
        h = layer(jnp.full((1,), l, jnp.int32), swa_sinks, h, cos_t, sin_t, row(pre_g), w_in_p,
                  row(gm_ln_g), row(gm_ln_b), gm_ws, bs_p, row(gla_bg), row(gng_p),
                  w_out_p, row(post_g))
    return h.reshape(batch, seq, D_MODEL)
```

```python
import functools
import math

import numpy as np
import jax
import jax.numpy as jnp
from jax import lax
from jax.experimental import pallas as pl
from jax.experimental.pallas import tpu as pltpu

F32 = jnp.float32
BF16 = jnp.bfloat16

D_MODEL = 1024
DEPTH = 4
HEAD_DIM = 64
NORM_EPS = 1e-6
GM_HEADS = 4
GM_WIDTH = 256
GLA_HEADS = 4
GLA_DV = 64
GLA_DK = 32
GLA_WIDTH = 256
GLA_KEY_WIDTH = 128
GLA_GATE_RANK = 16
GLA_TAU = 16.0
GLA_CHUNK = 16
SWA_Q_HEADS = 8
SWA_KV_HEADS = 2
SWA_GROUP = SWA_Q_HEADS // SWA_KV_HEADS
SWA_WIDTH = 512
SWA_KV_WIDTH = 128
ROPE_THETA = 10000.0
D_MIX = 1024
D_IN = 2832
LOG2E = math.log2(math.e)

BLK = 128
LANES = 128
SUBLANES = 8
TOKEN_BLOCK = 1024
VMEM_LIMIT_BYTES = 56 * 1024 * 1024
MIXER_STAGE_ORDER = ("L0", "G0", "L1", "S", "U0", "G1", "U1")

C_GM_U, C_GM_V, C_GM_Z = 0, 256, 512
C_GLA_Q, C_GLA_K, C_GLA_V, C_GLA_Z = 768, 896, 1024, 1280
C_SWA_Q, C_SWA_K, C_SWA_V, C_SWA_Z = 1536, 2048, 2304, 2432
C_GLA_GLR = 2944
D_IN_PAD = 3072
Y_A, Y_B, Y_C = 0, 256, 512


def _swa_out_order():
    r = np.arange(HEAD_DIM)
    return np.concatenate(
        [np.concatenate([64 * c + r, 64 * (SWA_GROUP + c) + r]) for c in range(SWA_GROUP)])


def _in_proj_columns():
    o = np.cumsum((0, 256, 256, 256, 128, 128, 256, 16, 256, 512, 128, 128, 512))
    (o_u, o_v, o_z, o_q, o_k, o_gv, o_glr, o_gz, o_sq, o_sk, o_sv, o_sz, _) = o
    half = HEAD_DIM // 2
    r = np.arange
    cols = [o_u + r(256), o_v + r(256), o_z + r(256),
            o_q + r(128), o_k + r(128), o_gv + r(256), o_gz + r(256)]
    for p in range(SWA_Q_HEADS // 2):
        a, b = o_sq + 64 * (2 * p), o_sq + 64 * (2 * p + 1)
        cols += [a + r(half), b + r(half), a + half + r(half), b + half + r(half)]
    for g in range(SWA_KV_HEADS):
        a = o_sk + 64 * g
        cols += [a + r(half), a + r(half), a + half + r(half), a + half + r(half)]
    cols += [o_sv + r(128)]
    cols += [_swa_out_order() + o_sz]
    cols += [o_glr + r(GLA_GATE_RANK), np.full(LANES - GLA_GATE_RANK, -1)]
    cols = np.concatenate(cols)
    assert cols.shape == (D_IN_PAD,)
    return cols


def _runs(idx):
    idx = np.asarray(idx)
    runs, start = [], 0
    for end in range(1, len(idx) + 1):
        if end < len(idx) and idx[end] == idx[end - 1] + 1 and idx[end - 1] >= 0:
            continue
        if end < len(idx) and idx[end] < 0 and idx[end - 1] < 0:
            continue
        runs.append((start, int(idx[start]), end - start))
        start = end
    return runs


def _relayout_rows_kernel(w_ref, o_ref, *, runs):
    for dst, src, n in runs:
        o_ref[dst:dst + n, :] = w_ref[src:src + n, :].astype(o_ref.dtype)


def _relayout_rows(w, idx):
    depth, r, c = w.shape
    return pl.pallas_call(
        functools.partial(_relayout_rows_kernel, runs=_runs(idx)),
        grid=(depth,),
        in_specs=[pl.BlockSpec((None, r, c), lambda l: (l, 0, 0))],
        out_specs=pl.BlockSpec((None, len(idx), c), lambda l: (l, 0, 0)),
        out_shape=jax.ShapeDtypeStruct((depth, len(idx), c), BF16),
        compiler_params=pltpu.CompilerParams(vmem_limit_bytes=VMEM_LIMIT_BYTES),
        name="weight_relayout_rows",
    )(w)


def _dot_hi_lo(a, b):
    a_hi, b_hi = a.astype(BF16), b.astype(BF16)
    a_lo = (a - a_hi.astype(F32)).astype(BF16)
    b_lo = (b - b_hi.astype(F32)).astype(BF16)
    return _dot(a_hi, b_hi) + _dot(a_hi, b_lo) + _dot(a_lo, b_hi)


def _relayout_t_kernel(wt_ref, wg2t_ref, o_ref, *, idx, block):
    for j in range(len(idx) // block):
        pieces = []
        for _, src, n in _runs(idx[j * block:(j + 1) * block]):
            pieces.append(jnp.zeros((n, wt_ref.shape[1]), F32) if src < 0 else wt_ref[src:src + n, :])
        rows_t = jnp.concatenate(pieces, axis=0)
        lo = C_GLA_GLR - j * block
        if 0 <= lo < block:
            folded = _dot_hi_lo(wg2t_ref[...], rows_t[lo:lo + LANES])
            parts = [rows_t[:lo], folded, rows_t[lo + LANES:]]
            rows_t = jnp.concatenate([p for p in parts if p.shape[0]], axis=0)
        o_ref[:, j * block:(j + 1) * block] = rows_t.T.astype(o_ref.dtype)


def _relayout_t(wt, wg2t, idx, block):
    depth, c, r = wt.shape
    assert len(idx) % block == 0
    return pl.pallas_call(
        functools.partial(_relayout_t_kernel, idx=np.asarray(idx), block=block),
        grid=(depth,),
        in_specs=[pl.BlockSpec((None, c, r), lambda l: (l, 0, 0)),
                  pl.BlockSpec((None, LANES, LANES), lambda l: (l, 0, 0))],
        out_specs=pl.BlockSpec((None, r, len(idx)), lambda l: (l, 0, 0)),
        out_shape=jax.ShapeDtypeStruct((depth, r, len(idx)), BF16),
        compiler_params=pltpu.CompilerParams(vmem_limit_bytes=VMEM_LIMIT_BYTES),
        name="weight_relayout_t",
    )(wt, wg2t)


def _silu(z):
    hz = 0.5 * z
    return hz + hz * jnp.tanh(hz)


def _dot(a, b):
    return jnp.dot(a, b, preferred_element_type=F32)


def _dot_nt(a, b):
    return lax.dot_general(a, b, (((1,), (1,)), ((), ())), preferred_element_type=F32)


def _rope_table_kernel(pos_ref, invf_ref, cos_ref, sin_ref):
    half = HEAD_DIM // 2
    ang = pos_ref[...] * invf_ref[...]
    cos_d, sin_d = jnp.cos(ang), jnp.sin(ang)
    lane = lax.broadcasted_iota(jnp.int32, ang.shape, 1)

    def spread(t, m):
        if m:
            t = pltpu.roll(t, LANES - half * m, 1)
        t = jnp.where(lane < half, t, 0.0)
        t = t + pltpu.roll(t, half, 1)
        return t + pltpu.roll(t, 2 * half, 1)

    for m in range(LANES // half):
        cos_ref[m] = spread(cos_d, m)
        s = spread(sin_d, m)
        sin_ref[m] = jnp.where(lane < LANES // 2, -s, s)


def _layer_kernel(layer_ref, sinks_ref, x_ref, cos_ref, sin_ref, pre_g_ref, w_in_ref, lng_ref, lnb_ref,
                  ws_ref, bs_ref, bg_ref, gng_ref, w_out_ref, post_g_ref,
                  o_ref, proj_ref, y_ref, state_ref, kprev_ref, vprev_ref,
                  gcum_ref, gk_ref, gv_ref, states_ref, ogla_ref, *, n_sub):
    step = pl.program_id(1)

    @pl.when(step == 0)
    def _():
        state_ref[...] = jnp.zeros_like(state_ref)
        kprev_ref[...] = jnp.zeros_like(kprev_ref)
        vprev_ref[...] = jnp.zeros_like(vprev_ref)
        ogla_ref[...] = jnp.zeros_like(ogla_ref)

    x = x_ref[...]
    h = x * lax.rsqrt(jnp.mean(x * x, axis=-1, keepdims=True) + NORM_EPS) * pre_g_ref[...]
    proj_ref[...] = _dot(h.astype(BF16), w_in_ref[...])

    def gla_finish(rows):
        o_gla = ogla_ref[...]
        seg_r = lax.broadcasted_iota(jnp.int32, (GLA_WIDTH, GLA_WIDTH), 0) // GLA_DV
        seg_c = lax.broadcasted_iota(jnp.int32, (GLA_WIDTH, GLA_WIDTH), 1) // GLA_DV
        head_ones = jnp.where(seg_r == seg_c, 1.0, 0.0).astype(BF16)
        sq = o_gla * o_gla
        sq_hi = sq.astype(BF16)
        sq_lo = (sq - sq_hi.astype(F32)).astype(BF16)
        sums = _dot(jnp.concatenate([sq_hi, sq_lo], axis=0), head_ones)
        ms = (sums[:BLK] + sums[BLK:]) * (1.0 / GLA_DV)
        yb = o_gla * lax.rsqrt(ms + NORM_EPS) * gng_ref[...]
        yb = yb * _silu(proj_ref[rows, C_GLA_Z:C_GLA_Z + GLA_WIDTH])
        y_ref[rows, Y_B:Y_B + GLA_WIDTH] = yb.astype(BF16)

    def sub_block(i, carry):
        rows = pl.ds(pl.multiple_of(i * BLK, BLK), BLK)
        row = lax.broadcasted_iota(jnp.int32, (BLK, BLK), 0)
        col = lax.broadcasted_iota(jnp.int32, (BLK, BLK), 1)
        lane = lax.broadcasted_iota(jnp.int32, (BLK, LANES), 1)
        n_chunk = BLK // GLA_CHUNK

        def seg(c0, width):
            return proj_ref[rows, c0:c0 + width]

        v = seg(C_GM_V, GM_WIDTH)
        mu = jnp.mean(v, axis=-1, keepdims=True)
        vc = v - mu
        var = jnp.mean(vc * vc, axis=-1, keepdims=True)

        logit = seg(C_GLA_GLR, LANES) + bg_ref[...]

        cosb = cos_ref[rows, :]
        sinb = sin_ref[rows, :]

        def rope(t):
            return t * cosb + pltpu.roll(t, LANES // 2, 1) * sinb

        lane2 = lax.broadcasted_iota(jnp.int32, (2 * BLK, LANES), 1)
        k_cur = [rope(seg(C_SWA_K + LANES * g, LANES)).astype(BF16) for g in range(SWA_KV_HEADS)]
        v_cur = seg(C_SWA_V, SWA_KV_WIDTH).astype(BF16)
        k_all = [jnp.concatenate([kprev_ref[g], k_cur[g]], axis=0)
                 for g in range(SWA_KV_HEADS)]
        v_all = jnp.concatenate([vprev_ref[...], v_cur], axis=0)
        one = jnp.ones((), BF16)
        v_aug = [jnp.where(lane2 < HEAD_DIM, v_all, one), jnp.where(lane2 < HEAD_DIM, one, v_all)]
        tq = lax.broadcasted_iota(jnp.int32, (BLK, 2 * BLK), 0)
        kc = lax.broadcasted_iota(jnp.int32, (BLK, 2 * BLK), 1)
        first_block = (step * n_sub + i) == 0
        prev_lo = jnp.where(first_block, BLK, 0)
        in_window = ((kc < BLK) & (kc > tq + prev_lo)) | ((kc >= BLK) & (kc - BLK <= tq))
        bias = jnp.where(in_window, 0.0, -jnp.inf)

        gla_finish(pl.ds(pl.multiple_of(jnp.maximum(i - 1, 0) * BLK, BLK), BLK))

        vln = (vc * lax.rsqrt(var + NORM_EPS) * lng_ref[...] + lnb_ref[...]).astype(BF16)
        w_tril = jnp.concatenate(
            [jnp.where(col <= row, ws_ref[hh], 0.0) for hh in range(GM_HEADS)], axis=0).astype(BF16)
        sv_all = _dot(w_tril, vln)
        sv_h = [sv_all[hh * BLK:(hh + 1) * BLK] for hh in range(GM_HEADS)]

        log_a = (jnp.minimum(logit, 0.0) - jnp.log1p(jnp.exp(-jnp.abs(logit)))) * (LOG2E / GLA_TAU)
        same_chunk = (row // GLA_CHUNK) == (col // GLA_CHUNK)
        tri16 = jnp.where(same_chunk & (col <= row), 1.0, 0.0).astype(BF16)
        ones16 = jnp.where(same_chunk, 1.0, 0.0).astype(BF16)
        la_hi = log_a.astype(BF16)
        la_lo = (log_a - la_hi.astype(F32)).astype(BF16)
        sums = _dot(jnp.concatenate([tri16, ones16], axis=0), jnp.concatenate([la_hi, la_lo], axis=1))
        g_cum = sums[:BLK, :LANES] + sums[:BLK, LANES:]
        g_end = sums[BLK:, :LANES] + sums[BLK:, LANES:]

        q_pairs = [rope(seg(C_SWA_Q + LANES * p, LANES)) * (HEAD_DIM ** -0.5 * LOG2E)
                   for p in range(SWA_Q_HEADS // 2)]
        lane_parity = (lane // (HEAD_DIM // 2)) % 2

        q = seg(C_GLA_Q, GLA_KEY_WIDTH) * (GLA_DK ** -0.5)
        k = seg(C_GLA_K, GLA_KEY_WIDTH)
        gv = seg(C_GLA_V, GLA_WIDTH)
        gcum_ref[...] = g_cum
        gk_ref[...] = k
        gv_ref[...] = gv
        hk = lax.broadcasted_iota(jnp.int32, (GLA_KEY_WIDTH, GLA_WIDTH), 0) // GLA_DK
        hv = lax.broadcasted_iota(jnp.int32, (GLA_KEY_WIDTH, GLA_WIDTH), 1) // GLA_DV
        head_match = hk == hv
        expand = jnp.where(head_match, 1.0, 0.0).astype(BF16)

        def state_updates():
            k_dec = k * jnp.exp2(g_end - g_cum)
            k_dec_t = k_dec.T
            lhs = jnp.concatenate(
                [jnp.where(col // GLA_CHUNK == jj, k_dec_t, 0.0) for jj in range(n_chunk)], axis=0)
            return _dot(lhs.astype(BF16), gv.astype(BF16))

        upd = state_updates()
        q_dec = q * jnp.exp2(g_cum)
        a_t = jnp.exp2(g_end).T

        t_local = row % GLA_CHUNK

        def chunk_rows(ref, s, n):
            return jnp.concatenate(
                [jnp.broadcast_to(ref[pl.ds(c * GLA_CHUNK + s, 1), :], (n, ref.shape[1]))
                 for c in range(n_chunk)], axis=0)

        def upper_half(t):
            return t.reshape(n_chunk, 2, SUBLANES, t.shape[1])[:, 1].reshape(BLK // 2, t.shape[1])

        q_up, g_up, t_up = upper_half(q), upper_half(g_cum), upper_half(t_local)

        def intra_terms(s_list):
            upper = s_list[0] >= SUBLANES
            qq, gg, tt, n = (q_up, g_up, t_up, SUBLANES) if upper else (q, g_cum, t_local, GLA_CHUNK)
            terms = []
            for s in s_list:
                term = qq * chunk_rows(gk_ref, s, n) * jnp.exp2(gg - chunk_rows(gcum_ref, s, n))
                terms.append(jnp.where(tt >= s, term, 0.0).astype(BF16))
            spread = _dot(jnp.concatenate(terms, axis=0), expand)
            m = qq.shape[0]
            out = spread[0:m] * chunk_rows(gv_ref, s_list[0], n)
            for i, s in enumerate(s_list[1:], 1):
                out = out + spread[i * m:(i + 1) * m] * chunk_rows(gv_ref, s, n)
            return out

        def attention_group(g):
            qm = []
            for hh in range(g * SWA_GROUP, (g + 1) * SWA_GROUP):
                p, par = divmod(hh, 2)
                qm.append(jnp.where(lane_parity == par, q_pairs[p], 0.0).astype(BF16))
            sc = _dot_nt(jnp.concatenate(qm, axis=0), k_all[g])
            es, sinks_w = [], []
            for i, hh in enumerate(range(g * SWA_GROUP, (g + 1) * SWA_GROUP)):
                sc_h = sc[i * BLK:(i + 1) * BLK] + bias
                sink = sinks_ref[layer_ref[0], hh] * LOG2E
                m = jnp.maximum(jnp.max(sc_h, axis=-1, keepdims=True), sink)
                es.append(jnp.exp2(sc_h - m).astype(BF16))
                sinks_w.append(jnp.exp2(sink - m))
            pv_all = _dot(jnp.concatenate(es, axis=0), v_aug[g])
            return [pv_all[i * BLK:(i + 1) * BLK] for i in range(SWA_GROUP)], sinks_w

        def state_pass():
            state = state_ref[...]
            for jj in range(n_chunk):
                states_ref[jj * BLK:(jj + 1) * BLK, :] = state.astype(BF16)
                a_j = a_t[:, jj * GLA_CHUNK:jj * GLA_CHUNK + 1]
                state = a_j * state + jnp.where(head_match, upd[jj * BLK:(jj + 1) * BLK], 0.0)
            state_ref[...] = state
            q_blk = jnp.concatenate(
                [jnp.where(row // GLA_CHUNK == jj, q_dec, 0.0) for jj in range(n_chunk)], axis=1)
            return _dot(q_blk.astype(BF16), states_ref[...])

        half = SUBLANES // 2
        key_rows = {"L0": range(0, half), "L1": range(half, SUBLANES),
                    "U0": range(SUBLANES, SUBLANES + half), "U1": range(SUBLANES + half, GLA_CHUNK)}
        o_lo = o_up = o_inter = None
        pv, sink_w = [None] * SWA_Q_HEADS, [None] * SWA_Q_HEADS
        for stage in MIXER_STAGE_ORDER:
            if stage[0] == "L":
                term = intra_terms(list(key_rows[stage]))
                o_lo = term if o_lo is None else o_lo + term
            elif stage[0] == "U":
                term = intra_terms(list(key_rows[stage]))
                o_up = term if o_up is None else o_up + term
            elif stage[0] == "G":
                g = int(stage[1])
                pv[g * SWA_GROUP:(g + 1) * SWA_GROUP], sink_w[g * SWA_GROUP:(g + 1) * SWA_GROUP] = attention_group(g)
            else:
                o_inter = state_pass()

        o_up = jnp.concatenate(
            [jnp.zeros((n_chunk, 1, SUBLANES, GLA_WIDTH), F32),
             o_up.reshape(n_chunk, 1, SUBLANES, GLA_WIDTH)], axis=1).reshape(BLK, GLA_WIDTH)
        ogla_ref[...] = o_lo + o_up + o_inter

        for c in range(SWA_GROUP):
            lo_head = lane < HEAD_DIM
            num = jnp.where(lo_head, pv[c], pv[SWA_GROUP + c])
            den = pltpu.roll(jnp.where(lo_head, pv[SWA_GROUP + c], pv[c]), LANES // 2, 1)
            den = den + jnp.where(lo_head, sink_w[c], sink_w[SWA_GROUP + c])
            yc = num / den * _silu(seg(C_SWA_Z + LANES * c, LANES))
            y_ref[rows, Y_C + LANES * c:Y_C + LANES * (c + 1)] = yc.astype(BF16)
        for g in range(SWA_KV_HEADS):
            kprev_ref[g] = k_cur[g]
        vprev_ref[...] = v_cur

        lane_head = lax.broadcasted_iota(jnp.int32, (BLK, GM_WIDTH), 1) // HEAD_DIM
        sv = sv_h[0]
        for hh in range(1, GM_HEADS):
            sv = jnp.where(lane_head == hh, sv_h[hh], sv)
        ya = seg(C_GM_U, GM_WIDTH) * (sv + bs_ref[...]) * _silu(seg(C_GM_Z, GM_WIDTH))
        y_ref[rows, Y_A:Y_A + GM_WIDTH] = ya.astype(BF16)
        return carry

    lax.fori_loop(0, n_sub, sub_block, 0)
    gla_finish(pl.ds((n_sub - 1) * BLK, BLK))

    y2 = _dot(y_ref[...], w_out_ref[...])
    y2 = y2 * lax.rsqrt(jnp.mean(y2 * y2, axis=-1, keepdims=True) + NORM_EPS) * post_g_ref[...]
    o_ref[...] = x_ref[...] + y2


def _rope_tables(positions):
    n = positions.size
    half = HEAD_DIM // 2
    inv_freq = jnp.power(ROPE_THETA, -jnp.arange(half, dtype=F32) * 2.0 / HEAD_DIM)
    per_row = LANES // half
    quarter = n // per_row
    invf = jnp.tile(inv_freq, per_row).reshape(1, LANES)
    pos = jnp.repeat(positions.reshape(per_row, quarter).T.astype(F32), half, axis=1)
    rows = 1024
    assert quarter % rows == 0
    out_spec = pl.BlockSpec((per_row, rows, LANES), lambda i: (0, i, 0))
    cos_t, sin_t = pl.pallas_call(
        _rope_table_kernel,
        grid=(quarter // rows,),
        in_specs=[pl.BlockSpec((rows, LANES), lambda i: (i, 0)), pl.BlockSpec((1, LANES), lambda i: (0, 0))],
        out_specs=[out_spec, out_spec],
        out_shape=[jax.ShapeDtypeStruct((per_row, quarter, LANES), F32)] * 2,
        name="rope_tables",
    )(pos, invf)
    return cos_t.reshape(n, LANES), sin_t.reshape(n, LANES)


def _layer_call(batch, seq):
    tb = TOKEN_BLOCK
    n_sub = tb // BLK
    steps = seq // tb
    tok = lambda b, j, l: (b * steps + j, 0)
    layer3 = lambda b, j, l: (l[0], 0, 0)
    layer4 = lambda b, j, l: (l[0], 0, 0, 0)
    in_specs = [
        pl.BlockSpec(memory_space=pltpu.SMEM),
        pl.BlockSpec((tb, D_MODEL), tok),
        pl.BlockSpec((tb, LANES), tok),
        pl.BlockSpec((tb, LANES), tok),
        pl.BlockSpec((None, 1, D_MODEL), layer3),
        pl.BlockSpec((None, D_MODEL, D_IN_PAD), layer3),
        pl.BlockSpec((None, 1, GM_WIDTH), layer3),
        pl.BlockSpec((None, 1, GM_WIDTH), layer3),
        pl.BlockSpec((None, GM_HEADS, BLK, BLK), layer4),
        pl.BlockSpec((None, BLK, GM_WIDTH), layer3),
        pl.BlockSpec((None, 1, GLA_KEY_WIDTH), layer3),
        pl.BlockSpec((None, 1, GLA_WIDTH), layer3),
        pl.BlockSpec((None, D_MIX, D_MODEL), layer3),
        pl.BlockSpec((None, 1, D_MODEL), layer3),
    ]
    return pl.pallas_call(
        functools.partial(_layer_kernel, n_sub=n_sub),
        grid_spec=pltpu.PrefetchScalarGridSpec(
            num_scalar_prefetch=1,
            grid=(batch, steps),
            in_specs=in_specs,
            out_specs=pl.BlockSpec((tb, D_MODEL), tok),
            scratch_shapes=[
                pltpu.VMEM((tb, D_IN_PAD), F32),
                pltpu.VMEM((tb, D_MIX), BF16),
                pltpu.VMEM((GLA_KEY_WIDTH, GLA_WIDTH), F32),
                pltpu.VMEM((SWA_KV_HEADS, BLK, LANES), BF16),
                pltpu.VMEM((BLK, SWA_KV_WIDTH), BF16),
                pltpu.VMEM((BLK, GLA_KEY_WIDTH), F32),
                pltpu.VMEM((BLK, GLA_KEY_WIDTH), F32),
                pltpu.VMEM((BLK, GLA_WIDTH), F32),
                pltpu.VMEM((BLK // GLA_CHUNK * GLA_KEY_WIDTH, GLA_WIDTH), BF16),
                pltpu.VMEM((BLK, GLA_WIDTH), F32),
            ]),
        out_shape=jax.ShapeDtypeStruct((batch * seq, D_MODEL), F32),
        compiler_params=pltpu.CompilerParams(
            dimension_semantics=("arbitrary", "arbitrary"),
            vmem_limit_bytes=VMEM_LIMIT_BYTES),
        name="hybrid_layer",
    )


def kernel(x, positions, pre_g, w_in, gm_ln_g, gm_ln_b, gm_ws, gm_bs, gla_wg2, gla_bg, gla_norm_g,
           swa_sinks, w_out, post_g):
    batch, seq, d_model = x.shape
    assert d_model == D_MODEL and seq % TOKEN_BLOCK == 0
    assert w_in.shape == (DEPTH, D_MODEL, D_IN) and w_out.shape == (DEPTH, D_MIX, D_MODEL)

    row = lambda p: p[:, None, :]
    wg2t = jnp.pad(jnp.swapaxes(gla_wg2, 1, 2), ((0, 0), (0, 0), (0, LANES - GLA_GATE_RANK)))
    w_in_p = _relayout_t(jnp.swapaxes(w_in, 1, 2), wg2t, _in_proj_columns(), block=2 * LANES)
    out_rows = np.concatenate([np.arange(Y_C), Y_C + _swa_out_order()])
    w_out_p = _relayout_rows(w_out, out_rows)
    bs_p = jnp.repeat(jnp.swapaxes(gm_bs, 1, 2), HEAD_DIM, axis=2)
    gng_p = jnp.tile(gla_norm_g, (1, GLA_HEADS))

    cos_t, sin_t = _rope_tables(positions)
    layer = _layer_call(batch, seq)
    h = x.reshape(batch * seq, D_MODEL)
    for l in range(DEPTH):
        h = layer(jnp.full((1,), l, jnp.int32), swa_sinks, h, cos_t, sin_t, row(pre_g), w_in_p,
                  row(gm_ln_g), row(gm_ln_b), gm_ws, bs_p, row(gla_bg), row(gng_p),
                  w_out_p, row(post_g))
    return h.reshape(batch, seq, D_MODEL)
```

```python
import functools
import math

import numpy as np
import jax
import jax.numpy as jnp
from jax import lax
from jax.experimental import pallas as pl
from jax.experimental.pallas import tpu as pltpu

F32 = jnp.float32
BF16 = jnp.bfloat16

D_MODEL = 1024
DEPTH = 4
HEAD_DIM = 64
NORM_EPS = 1e-6
GM_HEADS = 4
GM_WIDTH = 256
GLA_HEADS = 4
GLA_DV = 64
GLA_DK = 32
GLA_WIDTH = 256
GLA_KEY_WIDTH = 128
GLA_GATE_RANK = 16
GLA_TAU = 16.0
GLA_CHUNK = 16
SWA_Q_HEADS = 8
SWA_KV_HEADS = 2
SWA_GROUP = SWA_Q_HEADS // SWA_KV_HEADS
SWA_WIDTH = 512
SWA_KV_WIDTH = 128
ROPE_THETA = 10000.0
D_MIX = 1024
D_IN = 2832
LOG2E = math.log2(math.e)

BLK = 128
LANES = 128
SUBLANES = 8
TOKEN_BLOCK = 1024
VMEM_LIMIT_BYTES = 56 * 1024 * 1024
MIXER_STAGE_ORDER = ("L0", "G0", "L1", "S", "G1", "U0", "U1")

C_GM_U, C_GM_V, C_GM_Z = 0, 256, 512
C_GLA_Q, C_GLA_K, C_GLA_V, C_GLA_Z = 768, 896, 1024, 1280
C_SWA_Q, C_SWA_K, C_SWA_V, C_SWA_Z = 1536, 2048, 2304, 2432
C_GLA_GLR = 2944
D_IN_PAD = 3072
Y_A, Y_B, Y_C = 0, 256, 512


def _swa_out_order():
    r = np.arange(HEAD_DIM)
    return np.concatenate(
        [np.concatenate([64 * c + r, 64 * (SWA_GROUP + c) + r]) for c in range(SWA_GROUP)])


def _in_proj_columns():
    o = np.cumsum((0, 256, 256, 256, 128, 128, 256, 16, 256, 512, 128, 128, 512))
    (o_u, o_v, o_z, o_q, o_k, o_gv, o_glr, o_gz, o_sq, o_sk, o_sv, o_sz, _) = o
    half = HEAD_DIM // 2
    r = np.arange
    cols = [o_u + r(256), o_v + r(256), o_z + r(256),
            o_q + r(128), o_k + r(128), o_gv + r(256), o_gz + r(256)]
    for p in range(SWA_Q_HEADS // 2):
        a, b = o_sq + 64 * (2 * p), o_sq + 64 * (2 * p + 1)
        cols += [a + r(half), b + r(half), a + half + r(half), b + half + r(half)]
    for g in range(SWA_KV_HEADS):
        a = o_sk + 64 * g
        cols += [a + r(half), a + r(half), a + half + r(half), a + half + r(half)]
    cols += [o_sv + r(128)]
    cols += [_swa_out_order() + o_sz]
    cols += [o_glr + r(GLA_GATE_RANK), np.full(LANES - GLA_GATE_RANK, -1)]
    cols = np.concatenate(cols)
    assert cols.shape == (D_IN_PAD,)
    return cols


def _runs(idx):
    idx = np.asarray(idx)
    runs, start = [], 0
    for end in range(1, len(idx) + 1):
        if end < len(idx) and idx[end] == idx[end - 1] + 1 and idx[end - 1] >= 0:
            continue
        if end < len(idx) and idx[end] < 0 and idx[end - 1] < 0:
            continue
        runs.append((start, int(idx[start]), end - start))
        start = end
    return runs


def _relayout_rows_kernel(w_ref, o_ref, *, runs):
    for dst, src, n in runs:
        o_ref[dst:dst + n, :] = w_ref[src:src + n, :].astype(o_ref.dtype)


def _relayout_rows(w, idx):
    depth, r, c = w.shape
    return pl.pallas_call(
        functools.partial(_relayout_rows_kernel, runs=_runs(idx)),
        grid=(depth,),
        in_specs=[pl.BlockSpec((None, r, c), lambda l: (l, 0, 0))],
        out_specs=pl.BlockSpec((None, len(idx), c), lambda l: (l, 0, 0)),
        out_shape=jax.ShapeDtypeStruct((depth, len(idx), c), BF16),
        compiler_params=pltpu.CompilerParams(vmem_limit_bytes=VMEM_LIMIT_BYTES),
        name="weight_relayout_rows",
    )(w)


def _dot_hi_lo(a, b):
    a_hi, b_hi = a.astype(BF16), b.astype(BF16)
    a_lo = (a - a_hi.astype(F32)).astype(BF16)
    b_lo = (b - b_hi.astype(F32)).astype(BF16)
    return _dot(a_hi, b_hi) + _dot(a_hi, b_lo) + _dot(a_lo, b_hi)


def _relayout_t_kernel(wt_ref, wg2t_ref, o_ref, *, idx, block):
    for j in range(len(idx) // block):
        pieces = []
        for _, src, n in _runs(idx[j * block:(j + 1) * block]):
            pieces.append(jnp.zeros((n, wt_ref.shape[1]), F32) if src < 0 else wt_ref[src:src + n, :])
        rows_t = jnp.concatenate(pieces, axis=0)
        lo = C_GLA_GLR - j * block
        if 0 <= lo < block:
            folded = _dot_hi_lo(wg2t_ref[...], rows_t[lo:lo + LANES])
            parts = [rows_t[:lo], folded, rows_t[lo + LANES:]]
            rows_t = jnp.concatenate([p for p in parts if p.shape[0]], axis=0)
        o_ref[:, j * block:(j + 1) * block] = rows_t.T.astype(o_ref.dtype)


def _relayout_t(wt, wg2t, idx, block):
    depth, c, r = wt.shape
    assert len(idx) % block == 0
    return pl.pallas_call(
        functools.partial(_relayout_t_kernel, idx=np.asarray(idx), block=block),
        grid=(depth,),
        in_specs=[pl.BlockSpec((None, c, r), lambda l: (l, 0, 0)),
                  pl.BlockSpec((None, LANES, LANES), lambda l: (l, 0, 0))],
        out_specs=pl.BlockSpec((None, r, len(idx)), lambda l: (l, 0, 0)),
        out_shape=jax.ShapeDtypeStruct((depth, r, len(idx)), BF16),
        compiler_params=pltpu.CompilerParams(vmem_limit_bytes=VMEM_LIMIT_BYTES),
        name="weight_relayout_t",
    )(wt, wg2t)


def _silu(z):
    hz = 0.5 * z
    return hz + hz * jnp.tanh(hz)


def _dot(a, b):
    return jnp.dot(a, b, preferred_element_type=F32)


def _dot_nt(a, b):
    return lax.dot_general(a, b, (((1,), (1,)), ((), ())), preferred_element_type=F32)


def _rope_table_kernel(pos_ref, invf_ref, cos_ref, sin_ref):
    half = HEAD_DIM // 2
    ang = pos_ref[...] * invf_ref[...]
    cos_d, sin_d = jnp.cos(ang), jnp.sin(ang)
    lane = lax.broadcasted_iota(jnp.int32, ang.shape, 1)

    def spread(t, m):
        if m:
            t = pltpu.roll(t, LANES - half * m, 1)
        t = jnp.where(lane < half, t, 0.0)
        t = t + pltpu.roll(t, half, 1)
        return t + pltpu.roll(t, 2 * half, 1)

    for m in range(LANES // half):
        cos_ref[m] = spread(cos_d, m)
        s = spread(sin_d, m)
        sin_ref[m] = jnp.where(lane < LANES // 2, -s, s)


def _layer_kernel(layer_ref, sinks_ref, x_ref, cos_ref, sin_ref, pre_g_ref, w_in_ref, lng_ref, lnb_ref,
                  ws_ref, bs_ref, bg_ref, gng_ref, w_out_ref, post_g_ref,
                  o_ref, proj_ref, y_ref, state_ref, kprev_ref, vprev_ref,
                  gcum_ref, gk_ref, gv_ref, states_ref, ogla_ref, *, n_sub):
    step = pl.program_id(1)

    @pl.when(step == 0)
    def _():
        state_ref[...] = jnp.zeros_like(state_ref)
        kprev_ref[...] = jnp.zeros_like(kprev_ref)
        vprev_ref[...] = jnp.zeros_like(vprev_ref)
        ogla_ref[...] = jnp.zeros_like(ogla_ref)

    x = x_ref[...]
    h = x * lax.rsqrt(jnp.mean(x * x, axis=-1, keepdims=True) + NORM_EPS) * pre_g_ref[...]
    proj_ref[...] = _dot(h.astype(BF16), w_in_ref[...])

    def gla_finish(rows):
        o_gla = ogla_ref[...]
        seg_r = lax.broadcasted_iota(jnp.int32, (GLA_WIDTH, GLA_WIDTH), 0) // GLA_DV
        seg_c = lax.broadcasted_iota(jnp.int32, (GLA_WIDTH, GLA_WIDTH), 1) // GLA_DV
        head_ones = jnp.where(seg_r == seg_c, 1.0, 0.0).astype(BF16)
        sq = o_gla * o_gla
        sq_hi = sq.astype(BF16)
        sq_lo = (sq - sq_hi.astype(F32)).astype(BF16)
        sums = _dot(jnp.concatenate([sq_hi, sq_lo], axis=0), head_ones)
        ms = (sums[:BLK] + sums[BLK:]) * (1.0 / GLA_DV)
        yb = o_gla * lax.rsqrt(ms + NORM_EPS) * gng_ref[...]
        yb = yb * _silu(proj_ref[rows, C_GLA_Z:C_GLA_Z + GLA_WIDTH])
        y_ref[rows, Y_B:Y_B + GLA_WIDTH] = yb.astype(BF16)

    def sub_block(i, carry):
        rows = pl.ds(pl.multiple_of(i * BLK, BLK), BLK)
        row = lax.broadcasted_iota(jnp.int32, (BLK, BLK), 0)
        col = lax.broadcasted_iota(jnp.int32, (BLK, BLK), 1)
        lane = lax.broadcasted_iota(jnp.int32, (BLK, LANES), 1)
        n_chunk = BLK // GLA_CHUNK

        def seg(c0, width):
            return proj_ref[rows, c0:c0 + width]

        v = seg(C_GM_V, GM_WIDTH)
        mu = jnp.mean(v, axis=-1, keepdims=True)
        vc = v - mu
        var = jnp.mean(vc * vc, axis=-1, keepdims=True)

        logit = seg(C_GLA_GLR, LANES) + bg_ref[...]

        cosb = cos_ref[rows, :]
        sinb = sin_ref[rows, :]

        def rope(t):
            return t * cosb + pltpu.roll(t, LANES // 2, 1) * sinb

        lane2 = lax.broadcasted_iota(jnp.int32, (2 * BLK, LANES), 1)
        k_cur = [rope(seg(C_SWA_K + LANES * g, LANES)).astype(BF16) for g in range(SWA_KV_HEADS)]
        v_cur = seg(C_SWA_V, SWA_KV_WIDTH).astype(BF16)
        k_all = [jnp.concatenate([kprev_ref[g], k_cur[g]], axis=0)
                 for g in range(SWA_KV_HEADS)]
        v_all = jnp.concatenate([vprev_ref[...], v_cur], axis=0)
        one = jnp.ones((), BF16)
        v_aug = [jnp.where(lane2 < HEAD_DIM, v_all, one), jnp.where(lane2 < HEAD_DIM, one, v_all)]
        tq = lax.broadcasted_iota(jnp.int32, (BLK, 2 * BLK), 0)
        kc = lax.broadcasted_iota(jnp.int32, (BLK, 2 * BLK), 1)
        first_block = (step * n_sub + i) == 0
        prev_lo = jnp.where(first_block, BLK, 0)
        in_window = ((kc < BLK) & (kc > tq + prev_lo)) | ((kc >= BLK) & (kc - BLK <= tq))
        bias = jnp.where(in_window, 0.0, -jnp.inf)

        gla_finish(pl.ds(pl.multiple_of(jnp.maximum(i - 1, 0) * BLK, BLK), BLK))

        vln = (vc * lax.rsqrt(var + NORM_EPS) * lng_ref[...] + lnb_ref[...]).astype(BF16)
        w_tril = jnp.concatenate(
            [jnp.where(col <= row, ws_ref[hh], 0.0) for hh in range(GM_HEADS)], axis=0).astype(BF16)
        sv_all = _dot(w_tril, vln)
        sv_h = [sv_all[hh * BLK:(hh + 1) * BLK] for hh in range(GM_HEADS)]

        log_a = (jnp.minimum(logit, 0.0) - jnp.log1p(jnp.exp(-jnp.abs(logit)))) * (LOG2E / GLA_TAU)
        same_chunk = (row // GLA_CHUNK) == (col // GLA_CHUNK)
        tri16 = jnp.where(same_chunk & (col <= row), 1.0, 0.0).astype(BF16)
        ones16 = jnp.where(same_chunk, 1.0, 0.0).astype(BF16)
        la_hi = log_a.astype(BF16)
        la_lo = (log_a - la_hi.astype(F32)).astype(BF16)
        sums = _dot(jnp.concatenate([tri16, ones16], axis=0), jnp.concatenate([la_hi, la_lo], axis=1))
        g_cum = sums[:BLK, :LANES] + sums[:BLK, LANES:]
        g_end = sums[BLK:, :LANES] + sums[BLK:, LANES:]

        q_pairs = [rope(seg(C_SWA_Q + LANES * p, LANES)) * (HEAD_DIM ** -0.5 * LOG2E)
                   for p in range(SWA_Q_HEADS // 2)]
        lane_parity = (lane // (HEAD_DIM // 2)) % 2

        q = seg(C_GLA_Q, GLA_KEY_WIDTH) * (GLA_DK ** -0.5)
        k = seg(C_GLA_K, GLA_KEY_WIDTH)
        gv = seg(C_GLA_V, GLA_WIDTH)
        gcum_ref[...] = g_cum
        gk_ref[...] = k
        for t in range(GLA_WIDTH // LANES):
            gv_ref[t] = gv[:, t * LANES:(t + 1) * LANES]
        hk = lax.broadcasted_iota(jnp.int32, (GLA_KEY_WIDTH, GLA_WIDTH), 0) // GLA_DK
        hv = lax.broadcasted_iota(jnp.int32, (GLA_KEY_WIDTH, GLA_WIDTH), 1) // GLA_DV
        head_match = hk == hv
        expand = jnp.where(head_match, 1.0, 0.0).astype(BF16)

        def state_updates():
            k_dec = k * jnp.exp2(g_end - g_cum)
            k_dec_t = k_dec.T
            lhs = jnp.concatenate(
                [jnp.where(col // GLA_CHUNK == jj, k_dec_t, 0.0) for jj in range(n_chunk)], axis=0)
            return _dot(lhs.astype(BF16), gv.astype(BF16))

        upd = state_updates()
        q_dec = q * jnp.exp2(g_cum)
        a_t = jnp.exp2(g_end).T

        t_local = row % GLA_CHUNK

        def chunk_rows(ref, s, n):
            tiles = [ref] if len(ref.shape) == 2 else [ref.at[t] for t in range(ref.shape[0])]
            return jnp.concatenate(
                [jnp.concatenate([t[pl.ds(c * GLA_CHUNK + s, n, stride=0), :] for c in range(n_chunk)], axis=0)
                 for t in tiles], axis=1)

        def upper_half(t):
            return t.reshape(n_chunk, 2, SUBLANES, t.shape[1])[:, 1].reshape(BLK // 2, t.shape[1])

        q_up, g_up, t_up = upper_half(q), upper_half(g_cum), upper_half(t_local)

        def intra_terms(s_list):
            upper = s_list[0] >= SUBLANES
            qq, gg, tt, n = (q_up, g_up, t_up, SUBLANES) if upper else (q, g_cum, t_local, GLA_CHUNK)
            terms = []
            for s in s_list:
                term = qq * chunk_rows(gk_ref, s, n) * jnp.exp2(gg - chunk_rows(gcum_ref, s, n))
                terms.append(jnp.where(tt >= s, term, 0.0).astype(BF16))
            spread = _dot(jnp.concatenate(terms, axis=0), expand)
            m = qq.shape[0]
            out = spread[0:m] * chunk_rows(gv_ref, s_list[0], n)
            for i, s in enumerate(s_list[1:], 1):
                out = out + spread[i * m:(i + 1) * m] * chunk_rows(gv_ref, s, n)
            return out

        def attention_group(g):
            qm = []
            for hh in range(g * SWA_GROUP, (g + 1) * SWA_GROUP):
                p, par = divmod(hh, 2)
                qm.append(jnp.where(lane_parity == par, q_pairs[p], 0.0).astype(BF16))
            sc = _dot_nt(jnp.concatenate(qm, axis=0), k_all[g])
            es, sinks_w = [], []
            for i, hh in enumerate(range(g * SWA_GROUP, (g + 1) * SWA_GROUP)):
                sc_h = sc[i * BLK:(i + 1) * BLK] + bias
                sink = sinks_ref[layer_ref[0], hh] * LOG2E
                m = jnp.maximum(jnp.max(sc_h, axis=-1, keepdims=True), sink)
                es.append(jnp.exp2(sc_h - m).astype(BF16))
                sinks_w.append(jnp.exp2(sink - m))
            pv_all = _dot(jnp.concatenate(es, axis=0), v_aug[g])
            return [pv_all[i * BLK:(i + 1) * BLK] for i in range(SWA_GROUP)], sinks_w

        def state_pass():
            state = state_ref[...]
            for jj in range(n_chunk):
                states_ref[jj * BLK:(jj + 1) * BLK, :] = state.astype(BF16)
                a_j = a_t[:, jj * GLA_CHUNK:jj * GLA_CHUNK + 1]
                state = a_j * state + jnp.where(head_match, upd[jj * BLK:(jj + 1) * BLK], 0.0)
            state_ref[...] = state
            q_blk = jnp.concatenate(
                [jnp.where(row // GLA_CHUNK == jj, q_dec, 0.0) for jj in range(n_chunk)], axis=1)
            return _dot(q_blk.astype(BF16), states_ref[...])

        half = SUBLANES // 2
        key_rows = {"L0": range(0, half), "L1": range(half, SUBLANES),
                    "U0": range(SUBLANES, SUBLANES + half), "U1": range(SUBLANES + half, GLA_CHUNK)}
        o_lo = o_up = o_inter = None
        pv, sink_w = [None] * SWA_Q_HEADS, [None] * SWA_Q_HEADS
        for stage in MIXER_STAGE_ORDER:
            if stage[0] == "L":
                term = intra_terms(list(key_rows[stage]))
                o_lo = term if o_lo is None else o_lo + term
            elif stage[0] == "U":
                term = intra_terms(list(key_rows[stage]))
                o_up = term if o_up is None else o_up + term
            elif stage[0] == "G":
                g = int(stage[1])
                pv[g * SWA_GROUP:(g + 1) * SWA_GROUP], sink_w[g * SWA_GROUP:(g + 1) * SWA_GROUP] = attention_group(g)
            else:
                o_inter = state_pass()

        o_up = jnp.concatenate(
            [jnp.zeros((n_chunk, 1, SUBLANES, GLA_WIDTH), F32),
             o_up.reshape(n_chunk, 1, SUBLANES, GLA_WIDTH)], axis=1).reshape(BLK, GLA_WIDTH)
        ogla_ref[...] = o_lo + o_up + o_inter

        for c in range(SWA_GROUP):
            lo_head = lane < HEAD_DIM
            num = jnp.where(lo_head, pv[c], pv[SWA_GROUP + c])
            den = pltpu.roll(jnp.where(lo_head, pv[SWA_GROUP + c], pv[c]), LANES // 2, 1)
            den = den + jnp.where(lo_head, sink_w[c], sink_w[SWA_GROUP + c])
            yc = num / den * _silu(seg(C_SWA_Z + LANES * c, LANES))
            y_ref[rows, Y_C + LANES * c:Y_C + LANES * (c + 1)] = yc.astype(BF16)
        for g in range(SWA_KV_HEADS):
            kprev_ref[g] = k_cur[g]
        vprev_ref[...] = v_cur

        lane_head = lax.broadcasted_iota(jnp.int32, (BLK, GM_WIDTH), 1) // HEAD_DIM
        sv = sv_h[0]
        for hh in range(1, GM_HEADS):
            sv = jnp.where(lane_head == hh, sv_h[hh], sv)
        ya = seg(C_GM_U, GM_WIDTH) * (sv + bs_ref[...]) * _silu(seg(C_GM_Z, GM_WIDTH))
        y_ref[rows, Y_A:Y_A + GM_WIDTH] = ya.astype(BF16)
        return carry

    lax.fori_loop(0, n_sub, sub_block, 0)
    gla_finish(pl.ds((n_sub - 1) * BLK, BLK))

    y2 = _dot(y_ref[...], w_out_ref[...])
    y2 = y2 * lax.rsqrt(jnp.mean(y2 * y2, axis=-1, keepdims=True) + NORM_EPS) * post_g_ref[...]
    o_ref[...] = x_ref[...] + y2


def _rope_tables(positions):
    n = positions.size
    half = HEAD_DIM // 2
    inv_freq = jnp.power(ROPE_THETA, -jnp.arange(half, dtype=F32) * 2.0 / HEAD_DIM)
    per_row = LANES // half
    quarter = n // per_row
    invf = jnp.tile(inv_freq, per_row).reshape(1, LANES)
    pos = jnp.repeat(positions.reshape(per_row, quarter).T.astype(F32), half, axis=1)
    rows = 1024
    assert quarter % rows == 0
    out_spec = pl.BlockSpec((per_row, rows, LANES), lambda i: (0, i, 0))
    cos_t, sin_t = pl.pallas_call(
        _rope_table_kernel,
        grid=(quarter // rows,),
        in_specs=[pl.BlockSpec((rows, LANES), lambda i: (i, 0)), pl.BlockSpec((1, LANES), lambda i: (0, 0))],
        out_specs=[out_spec, out_spec],
        out_shape=[jax.ShapeDtypeStruct((per_row, quarter, LANES), F32)] * 2,
        name="rope_tables",
    )(pos, invf)
    return cos_t.reshape(n, LANES), sin_t.reshape(n, LANES)


def _layer_call(batch, seq):
    tb = TOKEN_BLOCK
    n_sub = tb // BLK
    steps = seq // tb
    tok = lambda b, j, l: (b * steps + j, 0)
    layer3 = lambda b, j, l: (l[0], 0, 0)
    layer4 = lambda b, j, l: (l[0], 0, 0, 0)
    in_specs = [
        pl.BlockSpec(memory_space=pltpu.SMEM),
        pl.BlockSpec((tb, D_MODEL), tok),
        pl.BlockSpec((tb, LANES), tok),
        pl.BlockSpec((tb, LANES), tok),
        pl.BlockSpec((None, 1, D_MODEL), layer3),
        pl.BlockSpec((None, D_MODEL, D_IN_PAD), layer3),
        pl.BlockSpec((None, 1, GM_WIDTH), layer3),
        pl.BlockSpec((None, 1, GM_WIDTH), layer3),
        pl.BlockSpec((None, GM_HEADS, BLK, BLK), layer4),
        pl.BlockSpec((None, BLK, GM_WIDTH), layer3),
        pl.BlockSpec((None, 1, GLA_KEY_WIDTH), layer3),
        pl.BlockSpec((None, 1, GLA_WIDTH), layer3),
        pl.BlockSpec((None, D_MIX, D_MODEL), layer3),
        pl.BlockSpec((None, 1, D_MODEL), layer3),
    ]
    return pl.pallas_call(
        functools.partial(_layer_kernel, n_sub=n_sub),
        grid_spec=pltpu.PrefetchScalarGridSpec(
            num_scalar_prefetch=1,
            grid=(batch, steps),
            in_specs=in_specs,
            out_specs=pl.BlockSpec((tb, D_MODEL), tok),
            scratch_shapes=[
                pltpu.VMEM((tb, D_IN_PAD), F32),
                pltpu.VMEM((tb, D_MIX), BF16),
                pltpu.VMEM((GLA_KEY_WIDTH, GLA_WIDTH), F32),
                pltpu.VMEM((SWA_KV_HEADS, BLK, LANES), BF16),
                pltpu.VMEM((BLK, SWA_KV_WIDTH), BF16),
                pltpu.VMEM((BLK, GLA_KEY_WIDTH), F32),
                pltpu.VMEM((BLK, GLA_KEY_WIDTH), F32),
                pltpu.VMEM((GLA_WIDTH // LANES, BLK, LANES), F32),
                pltpu.VMEM((BLK // GLA_CHUNK * GLA_KEY_WIDTH, GLA_WIDTH), BF16),
                pltpu.VMEM((BLK, GLA_WIDTH), F32),
            ]),
        out_shape=jax.ShapeDtypeStruct((batch * seq, D_MODEL), F32),
        compiler_params=pltpu.CompilerParams(
            dimension_semantics=("arbitrary", "arbitrary"),
            vmem_limit_bytes=VMEM_LIMIT_BYTES),
        name="hybrid_layer",
    )


def kernel(x, positions, pre_g, w_in, gm_ln_g, gm_ln_b, gm_ws, gm_bs, gla_wg2, gla_bg, gla_norm_g,
           swa_sinks, w_out, post_g):
    batch, seq, d_model = x.shape
    assert d_model == D_MODEL and seq % TOKEN_BLOCK == 0
    assert w_in.shape == (DEPTH, D_MODEL, D_IN) and w_out.shape == (DEPTH, D_MIX, D_MODEL)

    row = lambda p: p[:, None, :]
    wg2t = jnp.pad(jnp.swapaxes(gla_wg2, 1, 2), ((0, 0), (0, 0), (0, LANES - GLA_GATE_RANK)))
    w_in_p = _relayout_t(jnp.swapaxes(w_in, 1, 2), wg2t, _in_proj_columns(), block=2 * LANES)
    out_rows = np.concatenate([np.arange(Y_C), Y_C + _swa_out_order()])
    w_out_p = _relayout_rows(w_out, out_rows)
    bs_p = jnp.repeat(jnp.swapaxes(gm_bs, 1, 2), HEAD_DIM, axis=2)
    gng_p = jnp.tile(gla_norm_g, (1, GLA_HEADS))

    cos_t, sin_t = _rope_tables(positions)
    layer = _layer_call(batch, seq)
    h = x.reshape(batch * seq, D_MODEL)
    for l in range(DEPTH):
        h = layer(jnp.full((1,), l, jnp.int32), swa_sinks, h, cos_t, sin_t, row(pre_g), w_in_p,
                  row(gm_ln_g), row(gm_ln_b), gm_ws, bs_p, row(gla_bg), row(gng_p),
                  w_out_p, row(post_g))
    return h.reshape(batch, seq, D_MODEL)
```

```python
import functools
import math

import numpy as np
import jax
import jax.numpy as jnp
from jax import lax
from jax.experimental import pallas as pl
from jax.experimental.pallas import tpu as pltpu

F32 = jnp.float32
BF16 = jnp.bfloat16

D_MODEL = 1024
DEPTH = 4
HEAD_DIM = 64
NORM_EPS = 1e-6
GM_HEADS = 4
GM_WIDTH = 256
GLA_HEADS = 4
GLA_DV = 64
GLA_DK = 32
GLA_WIDTH = 256
GLA_KEY_WIDTH = 128
GLA_GATE_RANK = 16
GLA_TAU = 16.0
GLA_CHUNK = 16
SWA_Q_HEADS = 8
SWA_KV_HEADS = 2
SWA_GROUP = SWA_Q_HEADS // SWA_KV_HEADS
SWA_WIDTH = 512
SWA_KV_WIDTH = 128
ROPE_THETA = 10000.0
D_MIX = 1024
D_IN = 2832
LOG2E = math.log2(math.e)

BLK = 128
LANES = 128
SUBLANES = 8
TOKEN_BLOCK = 1024
VMEM_LIMIT_BYTES = 56 * 1024 * 1024
MIXER_STAGE_ORDER = ("L0", "G0", "L1", "S", "G1", "U0", "U1")

_IN_WIDTHS = (GM_WIDTH, GM_WIDTH, GM_WIDTH, GLA_KEY_WIDTH, GLA_KEY_WIDTH, GLA_WIDTH, GLA_WIDTH,
              SWA_WIDTH, 2 * SWA_KV_WIDTH, SWA_KV_WIDTH, SWA_WIDTH, GLA_KEY_WIDTH)
(C_GM_U, C_GM_V, C_GM_Z, C_GLA_Q, C_GLA_K, C_GLA_V, C_GLA_Z,
 C_SWA_Q, C_SWA_K, C_SWA_V, C_SWA_Z, C_GLA_GLR, D_IN_PAD) = (int(c) for c in np.cumsum((0,) + _IN_WIDTHS))
assert all(w % LANES == 0 for w in _IN_WIDTHS)
Y_A, Y_B, Y_C = 0, GM_WIDTH, GM_WIDTH + GLA_WIDTH
ROPE_ROWS_PER_STEP = 1024


def _swa_out_order():
    r = np.arange(HEAD_DIM)
    return np.concatenate(
        [np.concatenate([HEAD_DIM * c + r, HEAD_DIM * (SWA_GROUP + c) + r]) for c in range(SWA_GROUP)])


def _in_proj_columns():
    o = np.cumsum((0, GM_WIDTH, GM_WIDTH, GM_WIDTH, GLA_KEY_WIDTH, GLA_KEY_WIDTH, GLA_WIDTH, GLA_GATE_RANK,
                   GLA_WIDTH, SWA_WIDTH, SWA_KV_WIDTH, SWA_KV_WIDTH, SWA_WIDTH))
    (o_u, o_v, o_z, o_q, o_k, o_gv, o_glr, o_gz, o_sq, o_sk, o_sv, o_sz, end) = o
    assert end == D_IN
    half = HEAD_DIM // 2
    r = np.arange
    cols = [o_u + r(GM_WIDTH), o_v + r(GM_WIDTH), o_z + r(GM_WIDTH),
            o_q + r(GLA_KEY_WIDTH), o_k + r(GLA_KEY_WIDTH), o_gv + r(GLA_WIDTH), o_gz + r(GLA_WIDTH)]
    for p in range(SWA_Q_HEADS // 2):
        a, b = o_sq + HEAD_DIM * (2 * p), o_sq + HEAD_DIM * (2 * p + 1)
        cols += [a + r(half), b + r(half), a + half + r(half), b + half + r(half)]
    for g in range(SWA_KV_HEADS):
        a = o_sk + HEAD_DIM * g
        cols += [a + r(half), a + r(half), a + half + r(half), a + half + r(half)]
    cols += [o_sv + r(SWA_KV_WIDTH)]
    cols += [_swa_out_order() + o_sz]
    cols += [o_glr + r(GLA_GATE_RANK), np.full(LANES - GLA_GATE_RANK, -1)]
    cols = np.concatenate(cols)
    assert cols.shape == (D_IN_PAD,)
    return cols


def _runs(idx):
    idx = np.asarray(idx)
    runs, start = [], 0
    for end in range(1, len(idx) + 1):
        if end < len(idx) and idx[end] == idx[end - 1] + 1 and idx[end - 1] >= 0:
            continue
        if end < len(idx) and idx[end] < 0 and idx[end - 1] < 0:
            continue
        runs.append((start, int(idx[start]), end - start))
        start = end
    return runs


def _relayout_rows_kernel(w_ref, o_ref, *, runs):
    for dst, src, n in runs:
        o_ref[dst:dst + n, :] = w_ref[src:src + n, :].astype(o_ref.dtype)


def _relayout_rows(w, idx):
    depth, r, c = w.shape
    return pl.pallas_call(
        functools.partial(_relayout_rows_kernel, runs=_runs(idx)),
        grid=(depth,),
        in_specs=[pl.BlockSpec((None, r, c), lambda l: (l, 0, 0))],
        out_specs=pl.BlockSpec((None, len(idx), c), lambda l: (l, 0, 0)),
        out_shape=jax.ShapeDtypeStruct((depth, len(idx), c), BF16),
        compiler_params=pltpu.CompilerParams(vmem_limit_bytes=VMEM_LIMIT_BYTES),
        name="weight_relayout_rows",
    )(w)


def _dot_hi_lo(a, b):
    a_hi, b_hi = a.astype(BF16), b.astype(BF16)
    a_lo = (a - a_hi.astype(F32)).astype(BF16)
    b_lo = (b - b_hi.astype(F32)).astype(BF16)
    return _dot(a_hi, b_hi) + _dot(a_hi, b_lo) + _dot(a_lo, b_hi)


def _relayout_t_kernel(wt_ref, wg2t_ref, o_ref, *, idx, block):
    for j in range(len(idx) // block):
        pieces = []
        for _, src, n in _runs(idx[j * block:(j + 1) * block]):
            pieces.append(jnp.zeros((n, wt_ref.shape[1]), F32) if src < 0 else wt_ref[src:src + n, :])
        rows_t = jnp.concatenate(pieces, axis=0)
        lo = C_GLA_GLR - j * block
        if 0 <= lo < block:
            folded = _dot_hi_lo(wg2t_ref[...], rows_t[lo:lo + LANES])
            parts = [rows_t[:lo], folded, rows_t[lo + LANES:]]
            rows_t = jnp.concatenate([p for p in parts if p.shape[0]], axis=0)
        o_ref[:, j * block:(j + 1) * block] = rows_t.T.astype(o_ref.dtype)


def _relayout_t(wt, wg2t, idx, block):
    depth, c, r = wt.shape
    assert len(idx) % block == 0
    return pl.pallas_call(
        functools.partial(_relayout_t_kernel, idx=np.asarray(idx), block=block),
        grid=(depth,),
        in_specs=[pl.BlockSpec((None, c, r), lambda l: (l, 0, 0)),
                  pl.BlockSpec((None, LANES, LANES), lambda l: (l, 0, 0))],
        out_specs=pl.BlockSpec((None, r, len(idx)), lambda l: (l, 0, 0)),
        out_shape=jax.ShapeDtypeStruct((depth, r, len(idx)), BF16),
        compiler_params=pltpu.CompilerParams(vmem_limit_bytes=VMEM_LIMIT_BYTES),
        name="weight_relayout_t",
    )(wt, wg2t)


def _silu(z):
    hz = 0.5 * z
    return hz + hz * jnp.tanh(hz)


def _dot(a, b):
    return jnp.dot(a, b, preferred_element_type=F32)


def _dot_nt(a, b):
    return lax.dot_general(a, b, (((1,), (1,)), ((), ())), preferred_element_type=F32)


def _rope_table_kernel(pos_ref, invf_ref, cos_ref, sin_ref):
    half = HEAD_DIM // 2
    ang = pos_ref[...] * invf_ref[...]
    cos_d, sin_d = jnp.cos(ang), jnp.sin(ang)
    lane = lax.broadcasted_iota(jnp.int32, ang.shape, 1)

    def spread(t, m):
        if m:
            t = pltpu.roll(t, LANES - half * m, 1)
        t = jnp.where(lane < half, t, 0.0)
        t = t + pltpu.roll(t, half, 1)
        return t + pltpu.roll(t, 2 * half, 1)

    for m in range(LANES // half):
        cos_ref[m] = spread(cos_d, m)
        s = spread(sin_d, m)
        sin_ref[m] = jnp.where(lane < LANES // 2, -s, s)


def _layer_kernel(layer_ref, sinks_ref, x_ref, cos_ref, sin_ref, pre_g_ref, w_in_ref, lng_ref, lnb_ref,
                  ws_ref, bs_ref, bg_ref, gng_ref, w_out_ref, post_g_ref,
                  o_ref, proj_ref, y_ref, state_ref, kprev_ref, vprev_ref,
                  gcum_ref, gk_ref, gv_ref, states_ref, ogla_ref, *, n_sub):
    step = pl.program_id(1)

    @pl.when(step == 0)
    def _():
        state_ref[...] = jnp.zeros_like(state_ref)
        kprev_ref[...] = jnp.zeros_like(kprev_ref)
        vprev_ref[...] = jnp.zeros_like(vprev_ref)
        ogla_ref[...] = jnp.zeros_like(ogla_ref)

    x = x_ref[...]
    h = x * lax.rsqrt(jnp.mean(x * x, axis=-1, keepdims=True) + NORM_EPS) * pre_g_ref[...]
    proj_ref[...] = _dot(h.astype(BF16), w_in_ref[...])

    def gla_finish(rows):
        o_gla = ogla_ref[...]
        seg_r = lax.broadcasted_iota(jnp.int32, (GLA_WIDTH, GLA_WIDTH), 0) // GLA_DV
        seg_c = lax.broadcasted_iota(jnp.int32, (GLA_WIDTH, GLA_WIDTH), 1) // GLA_DV
        head_ones = jnp.where(seg_r == seg_c, 1.0, 0.0).astype(BF16)
        sq = o_gla * o_gla
        sq_hi = sq.astype(BF16)
        sq_lo = (sq - sq_hi.astype(F32)).astype(BF16)
        sums = _dot(jnp.concatenate([sq_hi, sq_lo], axis=0), head_ones)
        ms = (sums[:BLK] + sums[BLK:]) * (1.0 / GLA_DV)
        yb = o_gla * lax.rsqrt(ms + NORM_EPS) * gng_ref[...]
        yb = yb * _silu(proj_ref[rows, C_GLA_Z:C_GLA_Z + GLA_WIDTH])
        y_ref[rows, Y_B:Y_B + GLA_WIDTH] = yb.astype(BF16)

    def sub_block(i, carry):
        rows = pl.ds(pl.multiple_of(i * BLK, BLK), BLK)
        row = lax.broadcasted_iota(jnp.int32, (BLK, BLK), 0)
        col = lax.broadcasted_iota(jnp.int32, (BLK, BLK), 1)
        lane = lax.broadcasted_iota(jnp.int32, (BLK, LANES), 1)
        n_chunk = BLK // GLA_CHUNK

        def seg(c0, width):
            return proj_ref[rows, c0:c0 + width]

        v = seg(C_GM_V, GM_WIDTH)
        mu = jnp.mean(v, axis=-1, keepdims=True)
        vc = v - mu
        var = jnp.mean(vc * vc, axis=-1, keepdims=True)

        logit = seg(C_GLA_GLR, LANES) + bg_ref[...]

        cosb = cos_ref[rows, :]
        sinb = sin_ref[rows, :]

        def rope(t):
            return t * cosb + pltpu.roll(t, LANES // 2, 1) * sinb

        lane2 = lax.broadcasted_iota(jnp.int32, (2 * BLK, LANES), 1)
        k_cur = [rope(seg(C_SWA_K + LANES * g, LANES)).astype(BF16) for g in range(SWA_KV_HEADS)]
        v_cur = seg(C_SWA_V, SWA_KV_WIDTH).astype(BF16)
        k_all = [jnp.concatenate([kprev_ref[g], k_cur[g]], axis=0)
                 for g in range(SWA_KV_HEADS)]
        v_all = jnp.concatenate([vprev_ref[...], v_cur], axis=0)
        one = jnp.ones((), BF16)
        v_aug = [jnp.where(lane2 < HEAD_DIM, v_all, one), jnp.where(lane2 < HEAD_DIM, one, v_all)]
        tq = lax.broadcasted_iota(jnp.int32, (BLK, 2 * BLK), 0)
        kc = lax.broadcasted_iota(jnp.int32, (BLK, 2 * BLK), 1)
        first_block = (step * n_sub + i) == 0
        prev_lo = jnp.where(first_block, BLK, 0)
        in_window = ((kc < BLK) & (kc > tq + prev_lo)) | ((kc >= BLK) & (kc - BLK <= tq))
        bias = jnp.where(in_window, 0.0, -jnp.inf)

        gla_finish(pl.ds(pl.multiple_of(jnp.maximum(i - 1, 0) * BLK, BLK), BLK))

        vln = (vc * lax.rsqrt(var + NORM_EPS) * lng_ref[...] + lnb_ref[...]).astype(BF16)
        w_tril = jnp.concatenate(
            [jnp.where(col <= row, ws_ref[hh], 0.0) for hh in range(GM_HEADS)], axis=0).astype(BF16)
        sv_all = _dot(w_tril, vln)
        sv_h = [sv_all[hh * BLK:(hh + 1) * BLK] for hh in range(GM_HEADS)]

        log_a = (jnp.minimum(logit, 0.0) - jnp.log1p(jnp.exp(-jnp.abs(logit)))) * (LOG2E / GLA_TAU)
        same_chunk = (row // GLA_CHUNK) == (col // GLA_CHUNK)
        tri16 = jnp.where(same_chunk & (col <= row), 1.0, 0.0).astype(BF16)
        ones16 = jnp.where(same_chunk, 1.0, 0.0).astype(BF16)
        la_hi = log_a.astype(BF16)
        la_lo = (log_a - la_hi.astype(F32)).astype(BF16)
        sums = _dot(jnp.concatenate([tri16, ones16], axis=0), jnp.concatenate([la_hi, la_lo], axis=1))
        g_cum = sums[:BLK, :LANES] + sums[:BLK, LANES:]
        g_end = sums[BLK:, :LANES] + sums[BLK:, LANES:]

        q_pairs = [rope(seg(C_SWA_Q + LANES * p, LANES)) * (HEAD_DIM ** -0.5 * LOG2E)
                   for p in range(SWA_Q_HEADS // 2)]
        lane_parity = (lane // (HEAD_DIM // 2)) % 2

        q = seg(C_GLA_Q, GLA_KEY_WIDTH) * (GLA_DK ** -0.5)
        k = seg(C_GLA_K, GLA_KEY_WIDTH)
        gv = seg(C_GLA_V, GLA_WIDTH)
        gcum_ref[...] = g_cum
        gk_ref[...] = k
        for t in range(GLA_WIDTH // LANES):
            gv_ref[t] = gv[:, t * LANES:(t + 1) * LANES]
        hk = lax.broadcasted_iota(jnp.int32, (GLA_KEY_WIDTH, GLA_WIDTH), 0) // GLA_DK
        hv = lax.broadcasted_iota(jnp.int32, (GLA_KEY_WIDTH, GLA_WIDTH), 1) // GLA_DV
        head_match = hk == hv
        expand = jnp.where(head_match, 1.0, 0.0).astype(BF16)

        def state_updates():
            k_dec = k * jnp.exp2(g_end - g_cum)
            k_dec_t = k_dec.T
            lhs = jnp.concatenate(
                [jnp.where(col // GLA_CHUNK == jj, k_dec_t, 0.0) for jj in range(n_chunk)], axis=0)
            return _dot(lhs.astype(BF16), gv.astype(BF16))

        upd = state_updates()
        q_dec = q * jnp.exp2(g_cum)
        a_t = jnp.exp2(g_end).T

        t_local = row % GLA_CHUNK

        def chunk_rows(ref, s, n):
            tiles = [ref] if len(ref.shape) == 2 else [ref.at[t] for t in range(ref.shape[0])]
            return jnp.concatenate(
                [jnp.concatenate([t[pl.ds(c * GLA_CHUNK + s, n, stride=0), :] for c in range(n_chunk)], axis=0)
                 for t in tiles], axis=1)

        def upper_half(t):
            return t.reshape(n_chunk, 2, SUBLANES, t.shape[1])[:, 1].reshape(BLK // 2, t.shape[1])

        q_up, g_up, t_up = upper_half(q), upper_half(g_cum), upper_half(t_local)

        def intra_terms(s_list):
            upper = s_list[0] >= SUBLANES
            qq, gg, tt, n = (q_up, g_up, t_up, SUBLANES) if upper else (q, g_cum, t_local, GLA_CHUNK)
            terms = []
            for s in s_list:
                term = qq * chunk_rows(gk_ref, s, n) * jnp.exp2(gg - chunk_rows(gcum_ref, s, n))
                terms.append(jnp.where(tt >= s, term, 0.0).astype(BF16))
            spread = _dot(jnp.concatenate(terms, axis=0), expand)
            m = qq.shape[0]
            out = spread[0:m] * chunk_rows(gv_ref, s_list[0], n)
            for i, s in enumerate(s_list[1:], 1):
                out = out + spread[i * m:(i + 1) * m] * chunk_rows(gv_ref, s, n)
            return out

        def attention_group(g):
            qm = []
            for hh in range(g * SWA_GROUP, (g + 1) * SWA_GROUP):
                p, par = divmod(hh, 2)
                qm.append(jnp.where(lane_parity == par, q_pairs[p], 0.0).astype(BF16))
            sc = _dot_nt(jnp.concatenate(qm, axis=0), k_all[g])
            es, sinks_w = [], []
            for i, hh in enumerate(range(g * SWA_GROUP, (g + 1) * SWA_GROUP)):
                sc_h = sc[i * BLK:(i + 1) * BLK] + bias
                sink = sinks_ref[layer_ref[0], hh] * LOG2E
                m = jnp.maximum(jnp.max(sc_h, axis=-1, keepdims=True), sink)
                es.append(jnp.exp2(sc_h - m).astype(BF16))
                sinks_w.append(jnp.exp2(sink - m))
            pv_all = _dot(jnp.concatenate(es, axis=0), v_aug[g])
            return [pv_all[i * BLK:(i + 1) * BLK] for i in range(SWA_GROUP)], sinks_w

        def state_pass():
            state = state_ref[...]
            for jj in range(n_chunk):
                states_ref[jj * BLK:(jj + 1) * BLK, :] = state.astype(BF16)
                a_j = a_t[:, jj * GLA_CHUNK:jj * GLA_CHUNK + 1]
                state = a_j * state + jnp.where(head_match, upd[jj * BLK:(jj + 1) * BLK], 0.0)
            state_ref[...] = state
            q_blk = jnp.concatenate(
                [jnp.where(row // GLA_CHUNK == jj, q_dec, 0.0) for jj in range(n_chunk)], axis=1)
            return _dot(q_blk.astype(BF16), states_ref[...])

        half = SUBLANES // 2
        key_rows = {"L0": range(0, half), "L1": range(half, SUBLANES),
                    "U0": range(SUBLANES, SUBLANES + half), "U1": range(SUBLANES + half, GLA_CHUNK)}
        o_lo = o_up = o_inter = None
        pv, sink_w = [None] * SWA_Q_HEADS, [None] * SWA_Q_HEADS
        for stage in MIXER_STAGE_ORDER:
            if stage[0] == "L":
                term = intra_terms(list(key_rows[stage]))
                o_lo = term if o_lo is None else o_lo + term
            elif stage[0] == "U":
                term = intra_terms(list(key_rows[stage]))
                o_up = term if o_up is None else o_up + term
            elif stage[0] == "G":
                g = int(stage[1])
                pv[g * SWA_GROUP:(g + 1) * SWA_GROUP], sink_w[g * SWA_GROUP:(g + 1) * SWA_GROUP] = attention_group(g)
            else:
                o_inter = state_pass()

        o_up = jnp.concatenate(
            [jnp.zeros((n_chunk, 1, SUBLANES, GLA_WIDTH), F32),
             o_up.reshape(n_chunk, 1, SUBLANES, GLA_WIDTH)], axis=1).reshape(BLK, GLA_WIDTH)
        ogla_ref[...] = o_lo + o_up + o_inter

        for c in range(SWA_GROUP):
            lo_head = lane < HEAD_DIM
            num = jnp.where(lo_head, pv[c], pv[SWA_GROUP + c])
            den = pltpu.roll(jnp.where(lo_head, pv[SWA_GROUP + c], pv[c]), LANES // 2, 1)
            den = den + jnp.where(lo_head, sink_w[c], sink_w[SWA_GROUP + c])
            yc = num / den * _silu(seg(C_SWA_Z + LANES * c, LANES))
            y_ref[rows, Y_C + LANES * c:Y_C + LANES * (c + 1)] = yc.astype(BF16)
        for g in range(SWA_KV_HEADS):
            kprev_ref[g] = k_cur[g]
        vprev_ref[...] = v_cur

        lane_head = lax.broadcasted_iota(jnp.int32, (BLK, GM_WIDTH), 1) // HEAD_DIM
        sv = sv_h[0]
        for hh in range(1, GM_HEADS):
            sv = jnp.where(lane_head == hh, sv_h[hh], sv)
        ya = seg(C_GM_U, GM_WIDTH) * (sv + bs_ref[...]) * _silu(seg(C_GM_Z, GM_WIDTH))
        y_ref[rows, Y_A:Y_A + GM_WIDTH] = ya.astype(BF16)
        return carry

    lax.fori_loop(0, n_sub, sub_block, 0)
    gla_finish(pl.ds((n_sub - 1) * BLK, BLK))

    y2 = _dot(y_ref[...], w_out_ref[...])
    y2 = y2 * lax.rsqrt(jnp.mean(y2 * y2, axis=-1, keepdims=True) + NORM_EPS) * post_g_ref[...]
    o_ref[...] = x_ref[...] + y2


def _rope_tables(positions):
    n = positions.size
    half = HEAD_DIM // 2
    inv_freq = jnp.power(ROPE_THETA, -jnp.arange(half, dtype=F32) * 2.0 / HEAD_DIM)
    per_row = LANES // half
    quarter = n // per_row
    invf = jnp.tile(inv_freq, per_row).reshape(1, LANES)
    pos = jnp.repeat(positions.reshape(per_row, quarter).T.astype(F32), half, axis=1)
    rows = ROPE_ROWS_PER_STEP
    assert quarter % rows == 0
    out_spec = pl.BlockSpec((per_row, rows, LANES), lambda i: (0, i, 0))
    cos_t, sin_t = pl.pallas_call(
        _rope_table_kernel,
        grid=(quarter // rows,),
        in_specs=[pl.BlockSpec((rows, LANES), lambda i: (i, 0)), pl.BlockSpec((1, LANES), lambda i: (0, 0))],
        out_specs=[out_spec, out_spec],
        out_shape=[jax.ShapeDtypeStruct((per_row, quarter, LANES), F32)] * 2,
        name="rope_tables",
    )(pos, invf)
    return cos_t.reshape(n, LANES), sin_t.reshape(n, LANES)


def _layer_call(batch, seq):
    tb = TOKEN_BLOCK
    n_sub = tb // BLK
    steps = seq // tb
    tok = lambda b, j, l: (b * steps + j, 0)
    layer3 = lambda b, j, l: (l[0], 0, 0)
    layer4 = lambda b, j, l: (l[0], 0, 0, 0)
    in_specs = [
        pl.BlockSpec(memory_space=pltpu.SMEM),
        pl.BlockSpec((tb, D_MODEL), tok),
        pl.BlockSpec((tb, LANES), tok),
        pl.BlockSpec((tb, LANES), tok),
        pl.BlockSpec((None, 1, D_MODEL), layer3),
        pl.BlockSpec((None, D_MODEL, D_IN_PAD), layer3),
        pl.BlockSpec((None, 1, GM_WIDTH), layer3),
        pl.BlockSpec((None, 1, GM_WIDTH), layer3),
        pl.BlockSpec((None, GM_HEADS, BLK, BLK), layer4),
        pl.BlockSpec((None, BLK, GM_WIDTH), layer3),
        pl.BlockSpec((None, 1, GLA_KEY_WIDTH), layer3),
        pl.BlockSpec((None, 1, GLA_WIDTH), layer3),
        pl.BlockSpec((None, D_MIX, D_MODEL), layer3),
        pl.BlockSpec((None, 1, D_MODEL), layer3),
    ]
    return pl.pallas_call(
        functools.partial(_layer_kernel, n_sub=n_sub),
        grid_spec=pltpu.PrefetchScalarGridSpec(
            num_scalar_prefetch=1,
            grid=(batch, steps),
            in_specs=in_specs,
            out_specs=pl.BlockSpec((tb, D_MODEL), tok),
            scratch_shapes=[
                pltpu.VMEM((tb, D_IN_PAD), F32),
                pltpu.VMEM((tb, D_MIX), BF16),
                pltpu.VMEM((GLA_KEY_WIDTH, GLA_WIDTH), F32),
                pltpu.VMEM((SWA_KV_HEADS, BLK, LANES), BF16),
                pltpu.VMEM((BLK, SWA_KV_WIDTH), BF16),
                pltpu.VMEM((BLK, GLA_KEY_WIDTH), F32),
                pltpu.VMEM((BLK, GLA_KEY_WIDTH), F32),
                pltpu.VMEM((GLA_WIDTH // LANES, BLK, LANES), F32),
                pltpu.VMEM((BLK // GLA_CHUNK * GLA_KEY_WIDTH, GLA_WIDTH), BF16),
                pltpu.VMEM((BLK, GLA_WIDTH), F32),
            ]),
        out_shape=jax.ShapeDtypeStruct((batch * seq, D_MODEL), F32),
        compiler_params=pltpu.CompilerParams(
            dimension_semantics=("arbitrary", "arbitrary"),
            vmem_limit_bytes=VMEM_LIMIT_BYTES),
        name="hybrid_layer",
    )


def kernel(x, positions, pre_g, w_in, gm_ln_g, gm_ln_b, gm_ws, gm_bs, gla_wg2, gla_bg, gla_norm_g,
           swa_sinks, w_out, post_g):
    batch, seq, d_model = x.shape
    assert d_model == D_MODEL and seq % TOKEN_BLOCK == 0
    assert w_in.shape == (DEPTH, D_MODEL, D_IN) and w_out.shape == (DEPTH, D_MIX, D_MODEL)

    row = lambda p: p[:, None, :]
    wg2t = jnp.pad(jnp.swapaxes(gla_wg2, 1, 2), ((0, 0), (0, 0), (0, LANES - GLA_GATE_RANK)))
    w_in_p = _relayout_t(jnp.swapaxes(w_in, 1, 2), wg2t, _in_proj_columns(), block=2 * LANES)
    out_rows = np.concatenate([np.arange(Y_C), Y_C + _swa_out_order()])
    w_out_p = _relayout_rows(w_out, out_rows)
    bs_p = jnp.repeat(jnp.swapaxes(gm_bs, 1, 2), HEAD_DIM, axis=2)
    gng_p = jnp.tile(gla_norm_g, (1, GLA_HEADS))

    cos_t, sin_t = _rope_tables(positions)
    layer = _layer_call(batch, seq)
    h = x.reshape(batch * seq, D_MODEL)
    for l in range(DEPTH):
        h = layer(jnp.full((1,), l, jnp.int32), swa_sinks, h, cos_t, sin_t, row(pre_g), w_in_p,
                  row(gm_ln_g), row(gm_ln_b), gm_ws, bs_p, row(gla_bg), row(gng_p),
                  w_out_p, row(post_g))
    return h.reshape(batch, seq, D_MODEL)
```

```python
import functools
import math

import numpy as np
import jax
import jax.numpy as jnp
from jax import lax
from jax.experimental import pallas as pl
from jax.experimental.pallas import tpu as pltpu

F32 = jnp.float32
BF16 = jnp.bfloat16

D_MODEL = 1024
DEPTH = 4
HEAD_DIM = 64
NORM_EPS = 1e-6
GM_HEADS = 4
GM_WIDTH = 256
GLA_HEADS = 4
GLA_DV = 64
GLA_DK = 32
GLA_WIDTH = 256
GLA_KEY_WIDTH = 128
GLA_GATE_RANK = 16
GLA_TAU = 16.0
GLA_CHUNK = 16
SWA_Q_HEADS = 8
SWA_KV_HEADS = 2
SWA_GROUP = SWA_Q_HEADS // SWA_KV_HEADS
SWA_WIDTH = 512
SWA_KV_WIDTH = 128
ROPE_THETA = 10000.0
D_MIX = 1024
D_IN = 2832
LOG2E = math.log2(math.e)

BLK = 128
LANES = 128
SUBLANES = 8
TOKEN_BLOCK = 1024
VMEM_LIMIT_BYTES = 56 * 1024 * 1024
MIXER_STAGE_ORDER = ("L0", "G0", "L1", "S", "G1", "U0", "U1")

_IN_WIDTHS = (GM_WIDTH, GM_WIDTH, GM_WIDTH, GLA_KEY_WIDTH, GLA_KEY_WIDTH, GLA_WIDTH, GLA_WIDTH,
              SWA_WIDTH, 2 * SWA_KV_WIDTH, SWA_KV_WIDTH, SWA_WIDTH, GLA_KEY_WIDTH)
(C_GM_U, C_GM_V, C_GM_Z, C_GLA_Q, C_GLA_K, C_GLA_V, C_GLA_Z,
 C_SWA_Q, C_SWA_K, C_SWA_V, C_SWA_Z, C_GLA_GLR, D_IN_PAD) = (int(c) for c in np.cumsum((0,) + _IN_WIDTHS))
assert all(w % LANES == 0 for w in _IN_WIDTHS)
Y_A, Y_B, Y_C = 0, GM_WIDTH, GM_WIDTH + GLA_WIDTH
ROPE_ROWS_PER_STEP = 1024


def _swa_out_order():
    r = np.arange(HEAD_DIM)
    return np.concatenate(
        [np.concatenate([HEAD_DIM * c + r, HEAD_DIM * (SWA_GROUP + c) + r]) for c in range(SWA_GROUP)])


def _in_proj_columns():
    o = np.cumsum((0, GM_WIDTH, GM_WIDTH, GM_WIDTH, GLA_KEY_WIDTH, GLA_KEY_WIDTH, GLA_WIDTH, GLA_GATE_RANK,
                   GLA_WIDTH, SWA_WIDTH, SWA_KV_WIDTH, SWA_KV_WIDTH, SWA_WIDTH))
    (o_u, o_v, o_z, o_q, o_k, o_gv, o_glr, o_gz, o_sq, o_sk, o_sv, o_sz, end) = o
    assert end == D_IN
    half = HEAD_DIM // 2
    r = np.arange
    cols = [o_u + r(GM_WIDTH), o_v + r(GM_WIDTH), o_z + r(GM_WIDTH),
            o_q + r(GLA_KEY_WIDTH), o_k + r(GLA_KEY_WIDTH), o_gv + r(GLA_WIDTH), o_gz + r(GLA_WIDTH)]
    for p in range(SWA_Q_HEADS // 2):
        a, b = o_sq + HEAD_DIM * (2 * p), o_sq + HEAD_DIM * (2 * p + 1)
        cols += [a + r(half), b + r(half), a + half + r(half), b + half + r(half)]
    for g in range(SWA_KV_HEADS):
        a = o_sk + HEAD_DIM * g
        cols += [a + r(half), a + r(half), a + half + r(half), a + half + r(half)]
    cols += [o_sv + r(SWA_KV_WIDTH)]
    cols += [_swa_out_order() + o_sz]
    cols += [o_glr + r(GLA_GATE_RANK), np.full(LANES - GLA_GATE_RANK, -1)]
    cols = np.concatenate(cols)
    assert cols.shape == (D_IN_PAD,)
    return cols


def _runs(idx):
    idx = np.asarray(idx)
    runs, start = [], 0
    for end in range(1, len(idx) + 1):
        if end < len(idx) and idx[end] == idx[end - 1] + 1 and idx[end - 1] >= 0:
            continue
        if end < len(idx) and idx[end] < 0 and idx[end - 1] < 0:
            continue
        runs.append((start, int(idx[start]), end - start))
        start = end
    return runs


def _relayout_rows_kernel(w_ref, o_ref, *, runs):
    for dst, src, n in runs:
        o_ref[dst:dst + n, :] = w_ref[src:src + n, :].astype(o_ref.dtype)


def _relayout_rows(w, idx):
    depth, r, c = w.shape
    return pl.pallas_call(
        functools.partial(_relayout_rows_kernel, runs=_runs(idx)),
        grid=(depth,),
        in_specs=[pl.BlockSpec((None, r, c), lambda l: (l, 0, 0))],
        out_specs=pl.BlockSpec((None, len(idx), c), lambda l: (l, 0, 0)),
        out_shape=jax.ShapeDtypeStruct((depth, len(idx), c), BF16),
        compiler_params=pltpu.CompilerParams(vmem_limit_bytes=VMEM_LIMIT_BYTES),
        name="weight_relayout_rows",
    )(w)


def _dot_hi_lo(a, b):
    a_hi, b_hi = a.astype(BF16), b.astype(BF16)
    a_lo = (a - a_hi.astype(F32)).astype(BF16)
    b_lo = (b - b_hi.astype(F32)).astype(BF16)
    return _dot(a_hi, b_hi) + _dot(a_hi, b_lo) + _dot(a_lo, b_hi)


def _relayout_t_kernel(wt_ref, wg2t_ref, o_ref, *, idx, block):
    for j in range(len(idx) // block):
        pieces = []
        for _, src, n in _runs(idx[j * block:(j + 1) * block]):
            pieces.append(jnp.zeros((n, wt_ref.shape[1]), F32) if src < 0 else wt_ref[src:src + n, :])
        rows_t = jnp.concatenate(pieces, axis=0)
        lo = C_GLA_GLR - j * block
        if 0 <= lo < block:
            folded = _dot_hi_lo(wg2t_ref[...], rows_t[lo:lo + LANES])
            parts = [rows_t[:lo], folded, rows_t[lo + LANES:]]
            rows_t = jnp.concatenate([p for p in parts if p.shape[0]], axis=0)
        o_ref[:, j * block:(j + 1) * block] = rows_t.T.astype(o_ref.dtype)


def _relayout_t(wt, wg2t, idx, block):
    depth, c, r = wt.shape
    assert len(idx) % block == 0
    return pl.pallas_call(
        functools.partial(_relayout_t_kernel, idx=np.asarray(idx), block=block),
        grid=(depth,),
        in_specs=[pl.BlockSpec((None, c, r), lambda l: (l, 0, 0)),
                  pl.BlockSpec((None, LANES, LANES), lambda l: (l, 0, 0))],
        out_specs=pl.BlockSpec((None, r, len(idx)), lambda l: (l, 0, 0)),
        out_shape=jax.ShapeDtypeStruct((depth, r, len(idx)), BF16),
        compiler_params=pltpu.CompilerParams(vmem_limit_bytes=VMEM_LIMIT_BYTES),
        name="weight_relayout_t",
    )(wt, wg2t)


def _silu(z):
    hz = 0.5 * z
    return hz + hz * jnp.tanh(hz)


def _dot(a, b):
    return jnp.dot(a, b, preferred_element_type=F32)


def _dot_nt(a, b):
    return lax.dot_general(a, b, (((1,), (1,)), ((), ())), preferred_element_type=F32)


def _rope_table_kernel(pos_ref, invf_ref, cos_ref, sin_ref):
    half = HEAD_DIM // 2
    ang = pos_ref[...] * invf_ref[...]
    cos_d, sin_d = jnp.cos(ang), jnp.sin(ang)
    lane = lax.broadcasted_iota(jnp.int32, ang.shape, 1)

    def spread(t, m):
        if m:
            t = pltpu.roll(t, LANES - half * m, 1)
        t = jnp.where(lane < half, t, 0.0)
        t = t + pltpu.roll(t, half, 1)
        return t + pltpu.roll(t, 2 * half, 1)

    for m in range(LANES // half):
        cos_ref[m] = spread(cos_d, m)
        s = spread(sin_d, m)
        sin_ref[m] = jnp.where(lane < LANES // 2, -s, s)


def _layer_kernel(layer_ref, sinks_ref, x_ref, cos_ref, sin_ref, pre_g_ref, w_in_ref, lng_ref, lnb_ref,
                  ws_ref, bs_ref, bg_ref, gng_ref, w_out_ref, post_g_ref,
                  o_ref, proj_ref, y_ref, state_ref, kprev_ref, vprev_ref,
                  gcum_ref, gk_ref, gv_ref, states_ref, ogla_ref, *, n_sub):
    step = pl.program_id(1)

    @pl.when(step == 0)
    def _():
        state_ref[...] = jnp.zeros_like(state_ref)
        kprev_ref[...] = jnp.zeros_like(kprev_ref)
        vprev_ref[...] = jnp.zeros_like(vprev_ref)
        ogla_ref[...] = jnp.zeros_like(ogla_ref)

    def layer_row(ref):
        return ref[pl.ds(layer_ref[0], 1), :]

    gm_ln_g, gm_ln_b = layer_row(lng_ref), layer_row(lnb_ref)
    gla_bg, gla_norm_g = layer_row(bg_ref), layer_row(gng_ref)

    x = x_ref[...]
    h = x * lax.rsqrt(jnp.mean(x * x, axis=-1, keepdims=True) + NORM_EPS) * layer_row(pre_g_ref)
    proj_ref[...] = _dot(h.astype(BF16), w_in_ref[...])

    def gla_finish(rows):
        o_gla = ogla_ref[...]
        seg_r = lax.broadcasted_iota(jnp.int32, (GLA_WIDTH, GLA_WIDTH), 0) // GLA_DV
        seg_c = lax.broadcasted_iota(jnp.int32, (GLA_WIDTH, GLA_WIDTH), 1) // GLA_DV
        head_ones = jnp.where(seg_r == seg_c, 1.0, 0.0).astype(BF16)
        sq = o_gla * o_gla
        sq_hi = sq.astype(BF16)
        sq_lo = (sq - sq_hi.astype(F32)).astype(BF16)
        sums = _dot(jnp.concatenate([sq_hi, sq_lo], axis=0), head_ones)
        ms = (sums[:BLK] + sums[BLK:]) * (1.0 / GLA_DV)
        yb = o_gla * lax.rsqrt(ms + NORM_EPS) * gla_norm_g
        yb = yb * _silu(proj_ref[rows, C_GLA_Z:C_GLA_Z + GLA_WIDTH])
        y_ref[rows, Y_B:Y_B + GLA_WIDTH] = yb.astype(BF16)

    def sub_block(i, carry):
        rows = pl.ds(pl.multiple_of(i * BLK, BLK), BLK)
        row = lax.broadcasted_iota(jnp.int32, (BLK, BLK), 0)
        col = lax.broadcasted_iota(jnp.int32, (BLK, BLK), 1)
        lane = lax.broadcasted_iota(jnp.int32, (BLK, LANES), 1)
        n_chunk = BLK // GLA_CHUNK

        def seg(c0, width):
            return proj_ref[rows, c0:c0 + width]

        v = seg(C_GM_V, GM_WIDTH)
        mu = jnp.mean(v, axis=-1, keepdims=True)
        vc = v - mu
        var = jnp.mean(vc * vc, axis=-1, keepdims=True)

        logit = seg(C_GLA_GLR, LANES) + gla_bg

        cosb = cos_ref[rows, :]
        sinb = sin_ref[rows, :]

        def rope(t):
            return t * cosb + pltpu.roll(t, LANES // 2, 1) * sinb

        lane2 = lax.broadcasted_iota(jnp.int32, (2 * BLK, LANES), 1)
        k_cur = [rope(seg(C_SWA_K + LANES * g, LANES)).astype(BF16) for g in range(SWA_KV_HEADS)]
        v_cur = seg(C_SWA_V, SWA_KV_WIDTH).astype(BF16)
        k_all = [jnp.concatenate([kprev_ref[g], k_cur[g]], axis=0)
                 for g in range(SWA_KV_HEADS)]
        v_all = jnp.concatenate([vprev_ref[...], v_cur], axis=0)
        one = jnp.ones((), BF16)
        v_aug = [jnp.where(lane2 < HEAD_DIM, v_all, one), jnp.where(lane2 < HEAD_DIM, one, v_all)]
        tq = lax.broadcasted_iota(jnp.int32, (BLK, 2 * BLK), 0)
        kc = lax.broadcasted_iota(jnp.int32, (BLK, 2 * BLK), 1)
        first_block = (step * n_sub + i) == 0
        prev_lo = jnp.where(first_block, BLK, 0)
        in_window = ((kc < BLK) & (kc > tq + prev_lo)) | ((kc >= BLK) & (kc - BLK <= tq))
        bias = jnp.where(in_window, 0.0, -jnp.inf)

        gla_finish(pl.ds(pl.multiple_of(jnp.maximum(i - 1, 0) * BLK, BLK), BLK))

        vln = (vc * lax.rsqrt(var + NORM_EPS) * gm_ln_g + gm_ln_b).astype(BF16)
        w_tril = jnp.concatenate(
            [jnp.where(col <= row, ws_ref[hh], 0.0) for hh in range(GM_HEADS)], axis=0).astype(BF16)
        sv_all = _dot(w_tril, vln)
        sv_h = [sv_all[hh * BLK:(hh + 1) * BLK] for hh in range(GM_HEADS)]

        log_a = (jnp.minimum(logit, 0.0) - jnp.log1p(jnp.exp(-jnp.abs(logit)))) * (LOG2E / GLA_TAU)
        same_chunk = (row // GLA_CHUNK) == (col // GLA_CHUNK)
        tri16 = jnp.where(same_chunk & (col <= row), 1.0, 0.0).astype(BF16)
        ones16 = jnp.where(same_chunk, 1.0, 0.0).astype(BF16)
        la_hi = log_a.astype(BF16)
        la_lo = (log_a - la_hi.astype(F32)).astype(BF16)
        sums = _dot(jnp.concatenate([tri16, ones16], axis=0), jnp.concatenate([la_hi, la_lo], axis=1))
        g_cum = sums[:BLK, :LANES] + sums[:BLK, LANES:]
        g_end = sums[BLK:, :LANES] + sums[BLK:, LANES:]

        q_pairs = [rope(seg(C_SWA_Q + LANES * p, LANES)) * (HEAD_DIM ** -0.5 * LOG2E)
                   for p in range(SWA_Q_HEADS // 2)]
        lane_parity = (lane // (HEAD_DIM // 2)) % 2

        q = seg(C_GLA_Q, GLA_KEY_WIDTH) * (GLA_DK ** -0.5)
        k = seg(C_GLA_K, GLA_KEY_WIDTH)
        gv = seg(C_GLA_V, GLA_WIDTH)
        gcum_ref[...] = g_cum
        gk_ref[...] = k
        for t in range(GLA_WIDTH // LANES):
            gv_ref[t] = gv[:, t * LANES:(t + 1) * LANES]
        hk = lax.broadcasted_iota(jnp.int32, (GLA_KEY_WIDTH, GLA_WIDTH), 0) // GLA_DK
        hv = lax.broadcasted_iota(jnp.int32, (GLA_KEY_WIDTH, GLA_WIDTH), 1) // GLA_DV
        head_match = hk == hv
        expand = jnp.where(head_match, 1.0, 0.0).astype(BF16)

        def state_updates():
            k_dec = k * jnp.exp2(g_end - g_cum)
            k_dec_t = k_dec.T
            lhs = jnp.concatenate(
                [jnp.where(col // GLA_CHUNK == jj, k_dec_t, 0.0) for jj in range(n_chunk)], axis=0)
            return _dot(lhs.astype(BF16), gv.astype(BF16))

        upd = state_updates()
        q_dec = q * jnp.exp2(g_cum)
        a_t = jnp.exp2(g_end).T

        t_local = row % GLA_CHUNK

        def chunk_rows(ref, s, n):
            tiles = [ref] if len(ref.shape) == 2 else [ref.at[t] for t in range(ref.shape[0])]
            return jnp.concatenate(
                [jnp.concatenate([t[pl.ds(c * GLA_CHUNK + s, n, stride=0), :] for c in range(n_chunk)], axis=0)
                 for t in tiles], axis=1)

        def upper_half(t):
            return t.reshape(n_chunk, 2, SUBLANES, t.shape[1])[:, 1].reshape(BLK // 2, t.shape[1])

        q_up, g_up, t_up = upper_half(q), upper_half(g_cum), upper_half(t_local)

        def intra_terms(s_list):
            upper = s_list[0] >= SUBLANES
            qq, gg, tt, n = (q_up, g_up, t_up, SUBLANES) if upper else (q, g_cum, t_local, GLA_CHUNK)
            terms = []
            for s in s_list:
                term = qq * chunk_rows(gk_ref, s, n) * jnp.exp2(gg - chunk_rows(gcum_ref, s, n))
                terms.append(jnp.where(tt >= s, term, 0.0).astype(BF16))
            spread = _dot(jnp.concatenate(terms, axis=0), expand)
            m = qq.shape[0]
            out = spread[0:m] * chunk_rows(gv_ref, s_list[0], n)
            for i, s in enumerate(s_list[1:], 1):
                out = out + spread[i * m:(i + 1) * m] * chunk_rows(gv_ref, s, n)
            return out

        def attention_group(g):
            qm = []
            for hh in range(g * SWA_GROUP, (g + 1) * SWA_GROUP):
                p, par = divmod(hh, 2)
                qm.append(jnp.where(lane_parity == par, q_pairs[p], 0.0).astype(BF16))
            sc = _dot_nt(jnp.concatenate(qm, axis=0), k_all[g])
            es, sinks_w = [], []
            for i, hh in enumerate(range(g * SWA_GROUP, (g + 1) * SWA_GROUP)):
                sc_h = sc[i * BLK:(i + 1) * BLK] + bias
                sink = sinks_ref[layer_ref[0], hh] * LOG2E
                m = jnp.maximum(jnp.max(sc_h, axis=-1, keepdims=True), sink)
                es.append(jnp.exp2(sc_h - m).astype(BF16))
                sinks_w.append(jnp.exp2(sink - m))
            pv_all = _dot(jnp.concatenate(es, axis=0), v_aug[g])
            return [pv_all[i * BLK:(i + 1) * BLK] for i in range(SWA_GROUP)], sinks_w

        def state_pass():
            state = state_ref[...]
            for jj in range(n_chunk):
                states_ref[jj * BLK:(jj + 1) * BLK, :] = state.astype(BF16)
                a_j = a_t[:, jj * GLA_CHUNK:jj * GLA_CHUNK + 1]
                state = a_j * state + jnp.where(head_match, upd[jj * BLK:(jj + 1) * BLK], 0.0)
            state_ref[...] = state
            q_blk = jnp.concatenate(
                [jnp.where(row // GLA_CHUNK == jj, q_dec, 0.0) for jj in range(n_chunk)], axis=1)
            return _dot(q_blk.astype(BF16), states_ref[...])

        half = SUBLANES // 2
        key_rows = {"L0": range(0, half), "L1": range(half, SUBLANES),
                    "U0": range(SUBLANES, SUBLANES + half), "U1": range(SUBLANES + half, GLA_CHUNK)}
        o_lo = o_up = o_inter = None
        pv, sink_w = [None] * SWA_Q_HEADS, [None] * SWA_Q_HEADS
        for stage in MIXER_STAGE_ORDER:
            if stage[0] == "L":
                term = intra_terms(list(key_rows[stage]))
                o_lo = term if o_lo is None else o_lo + term
            elif stage[0] == "U":
                term = intra_terms(list(key_rows[stage]))
                o_up = term if o_up is None else o_up + term
            elif stage[0] == "G":
                g = int(stage[1])
                pv[g * SWA_GROUP:(g + 1) * SWA_GROUP], sink_w[g * SWA_GROUP:(g + 1) * SWA_GROUP] = attention_group(g)
            else:
                o_inter = state_pass()

        o_up = jnp.concatenate(
            [jnp.zeros((n_chunk, 1, SUBLANES, GLA_WIDTH), F32),
             o_up.reshape(n_chunk, 1, SUBLANES, GLA_WIDTH)], axis=1).reshape(BLK, GLA_WIDTH)
        ogla_ref[...] = o_lo + o_up + o_inter

        for c in range(SWA_GROUP):
            lo_head = lane < HEAD_DIM
            num = jnp.where(lo_head, pv[c], pv[SWA_GROUP + c])
            den = pltpu.roll(jnp.where(lo_head, pv[SWA_GROUP + c], pv[c]), LANES // 2, 1)
            den = den + jnp.where(lo_head, sink_w[c], sink_w[SWA_GROUP + c])
            yc = num / den * _silu(seg(C_SWA_Z + LANES * c, LANES))
            y_ref[rows, Y_C + LANES * c:Y_C + LANES * (c + 1)] = yc.astype(BF16)
        for g in range(SWA_KV_HEADS):
            kprev_ref[g] = k_cur[g]
        vprev_ref[...] = v_cur

        lane_head = lax.broadcasted_iota(jnp.int32, (BLK, GM_WIDTH), 1) // HEAD_DIM
        sv = sv_h[0]
        for hh in range(1, GM_HEADS):
            sv = jnp.where(lane_head == hh, sv_h[hh], sv)
        ya = seg(C_GM_U, GM_WIDTH) * (sv + bs_ref[...]) * _silu(seg(C_GM_Z, GM_WIDTH))
        y_ref[rows, Y_A:Y_A + GM_WIDTH] = ya.astype(BF16)
        return carry

    lax.fori_loop(0, n_sub, sub_block, 0)
    gla_finish(pl.ds((n_sub - 1) * BLK, BLK))

    y2 = _dot(y_ref[...], w_out_ref[...])
    y2 = y2 * lax.rsqrt(jnp.mean(y2 * y2, axis=-1, keepdims=True) + NORM_EPS) * layer_row(post_g_ref)
    o_ref[...] = x_ref[...] + y2


def _rope_tables(positions):
    n = positions.size
    half = HEAD_DIM // 2
    inv_freq = jnp.power(ROPE_THETA, -jnp.arange(half, dtype=F32) * 2.0 / HEAD_DIM)
    per_row = LANES // half
    quarter = n // per_row
    invf = jnp.tile(inv_freq, per_row).reshape(1, LANES)
    pos = jnp.repeat(positions.reshape(per_row, quarter).T.astype(F32), half, axis=1)
    rows = ROPE_ROWS_PER_STEP
    assert quarter % rows == 0
    out_spec = pl.BlockSpec((per_row, rows, LANES), lambda i: (0, i, 0))
    cos_t, sin_t = pl.pallas_call(
        _rope_table_kernel,
        grid=(quarter // rows,),
        in_specs=[pl.BlockSpec((rows, LANES), lambda i: (i, 0)), pl.BlockSpec((1, LANES), lambda i: (0, 0))],
        out_specs=[out_spec, out_spec],
        out_shape=[jax.ShapeDtypeStruct((per_row, quarter, LANES), F32)] * 2,
        name="rope_tables",
    )(pos, invf)
    return cos_t.reshape(n, LANES), sin_t.reshape(n, LANES)


def _layer_call(batch, seq):
    tb = TOKEN_BLOCK
    n_sub = tb // BLK
    steps = seq // tb
    tok = lambda b, j, l: (b * steps + j, 0)
    layer3 = lambda b, j, l: (l[0], 0, 0)
    layer4 = lambda b, j, l: (l[0], 0, 0, 0)
    stacked_rows = lambda width: pl.BlockSpec((DEPTH, width), lambda b, j, l: (0, 0))
    in_specs = [
        pl.BlockSpec(memory_space=pltpu.SMEM),
        pl.BlockSpec((tb, D_MODEL), tok),
        pl.BlockSpec((tb, LANES), tok),
        pl.BlockSpec((tb, LANES), tok),
        stacked_rows(D_MODEL),
        pl.BlockSpec((None, D_MODEL, D_IN_PAD), layer3),
        stacked_rows(GM_WIDTH),
        stacked_rows(GM_WIDTH),
        pl.BlockSpec((None, GM_HEADS, BLK, BLK), layer4),
        pl.BlockSpec((None, BLK, GM_WIDTH), layer3),
        stacked_rows(GLA_KEY_WIDTH),
        stacked_rows(GLA_WIDTH),
        pl.BlockSpec((None, D_MIX, D_MODEL), layer3),
        stacked_rows(D_MODEL),
    ]
    return pl.pallas_call(
        functools.partial(_layer_kernel, n_sub=n_sub),
        grid_spec=pltpu.PrefetchScalarGridSpec(
            num_scalar_prefetch=1,
            grid=(batch, steps),
            in_specs=in_specs,
            out_specs=pl.BlockSpec((tb, D_MODEL), tok),
            scratch_shapes=[
                pltpu.VMEM((tb, D_IN_PAD), F32),
                pltpu.VMEM((tb, D_MIX), BF16),
                pltpu.VMEM((GLA_KEY_WIDTH, GLA_WIDTH), F32),
                pltpu.VMEM((SWA_KV_HEADS, BLK, LANES), BF16),
                pltpu.VMEM((BLK, SWA_KV_WIDTH), BF16),
                pltpu.VMEM((BLK, GLA_KEY_WIDTH), F32),
                pltpu.VMEM((BLK, GLA_KEY_WIDTH), F32),
                pltpu.VMEM((GLA_WIDTH // LANES, BLK, LANES), F32),
                pltpu.VMEM((BLK // GLA_CHUNK * GLA_KEY_WIDTH, GLA_WIDTH), BF16),
                pltpu.VMEM((BLK, GLA_WIDTH), F32),
            ]),
        out_shape=jax.ShapeDtypeStruct((batch * seq, D_MODEL), F32),
        compiler_params=pltpu.CompilerParams(
            dimension_semantics=("arbitrary", "arbitrary"),
            vmem_limit_bytes=VMEM_LIMIT_BYTES),
        name="hybrid_layer",
    )


def kernel(x, positions, pre_g, w_in, gm_ln_g, gm_ln_b, gm_ws, gm_bs, gla_wg2, gla_bg, gla_norm_g,
           swa_sinks, w_out, post_g):
    batch, seq, d_model = x.shape
    assert d_model == D_MODEL and seq % TOKEN_BLOCK == 0
    assert w_in.shape == (DEPTH, D_MODEL, D_IN) and w_out.shape == (DEPTH, D_MIX, D_MODEL)

    wg2t = jnp.pad(jnp.swapaxes(gla_wg2, 1, 2), ((0, 0), (0, 0), (0, LANES - GLA_GATE_RANK)))
    w_in_p = _relayout_t(jnp.swapaxes(w_in, 1, 2), wg2t, _in_proj_columns(), block=2 * LANES)
    out_rows = np.concatenate([np.arange(Y_C), Y_C + _swa_out_order()])
    w_out_p = _relayout_rows(w_out, out_rows)
    bs_p = jnp.repeat(jnp.swapaxes(gm_bs, 1, 2), HEAD_DIM, axis=2)
    gng_p = jnp.tile(gla_norm_g, (1, GLA_HEADS))

    cos_t, sin_t = _rope_tables(positions)
    layer = _layer_call(batch, seq)
    h = x.reshape(batch * seq, D_MODEL)
    for l in range(DEPTH):
        h = layer(jnp.full((1,), l, jnp.int32), swa_sinks, h, cos_t, sin_t, pre_g, w_in_p,
                  gm_ln_g, gm_ln_b, gm_ws, bs_p, gla_bg, gng_p, w_out_p, post_g)
    return h.reshape(batch, seq, D_MODEL)
```

```python
import functools
import math

import numpy as np
import jax
import jax.numpy as jnp
from jax import lax
from jax.experimental import pallas as pl
from jax.experimental.pallas import tpu as pltpu

F32 = jnp.float32
BF16 = jnp.bfloat16

D_MODEL = 1024
DEPTH = 4
HEAD_DIM = 64
NORM_EPS = 1e-6
GM_HEADS = 4
GM_WIDTH = 256
GLA_HEADS = 4
GLA_DV = 64
GLA_DK = 32
GLA_WIDTH = 256
GLA_KEY_WIDTH = 128
GLA_GATE_RANK = 16
GLA_TAU = 16.0
GLA_CHUNK = 16
SWA_Q_HEADS = 8
SWA_KV_HEADS = 2
SWA_GROUP = SWA_Q_HEADS // SWA_KV_HEADS
SWA_WIDTH = 512
SWA_KV_WIDTH = 128
ROPE_THETA = 10000.0
D_MIX = 1024
D_IN = 2832
LOG2E = math.log2(math.e)

BLK = 128
LANES = 128
SUBLANES = 8
TOKEN_BLOCK = 1024
VMEM_LIMIT_BYTES = 56 * 1024 * 1024
MIXER_STAGE_ORDER = ("L0", "G0", "L1", "S", "U0", "G1", "U1")

_IN_WIDTHS = (GM_WIDTH, GM_WIDTH, GM_WIDTH, GLA_KEY_WIDTH, GLA_KEY_WIDTH, GLA_WIDTH, GLA_WIDTH,
              SWA_WIDTH, 2 * SWA_KV_WIDTH, SWA_KV_WIDTH, SWA_WIDTH, GLA_KEY_WIDTH)
(C_GM_U, C_GM_V, C_GM_Z, C_GLA_Q, C_GLA_K, C_GLA_V, C_GLA_Z,
 C_SWA_Q, C_SWA_K, C_SWA_V, C_SWA_Z, C_GLA_GLR, D_IN_PAD) = (int(c) for c in np.cumsum((0,) + _IN_WIDTHS))
assert all(w % LANES == 0 for w in _IN_WIDTHS)
Y_A, Y_B, Y_C = 0, GM_WIDTH, GM_WIDTH + GLA_WIDTH
ROPE_ROWS_PER_STEP = 1024


def _swa_out_order():
    r = np.arange(HEAD_DIM)
    return np.concatenate(
        [np.concatenate([HEAD_DIM * c + r, HEAD_DIM * (SWA_GROUP + c) + r]) for c in range(SWA_GROUP)])


def _in_proj_columns():
    o = np.cumsum((0, GM_WIDTH, GM_WIDTH, GM_WIDTH, GLA_KEY_WIDTH, GLA_KEY_WIDTH, GLA_WIDTH, GLA_GATE_RANK,
                   GLA_WIDTH, SWA_WIDTH, SWA_KV_WIDTH, SWA_KV_WIDTH, SWA_WIDTH))
    (o_u, o_v, o_z, o_q, o_k, o_gv, o_glr, o_gz, o_sq, o_sk, o_sv, o_sz, end) = o
    assert end == D_IN
    half = HEAD_DIM // 2
    r = np.arange
    cols = [o_u + r(GM_WIDTH), o_v + r(GM_WIDTH), o_z + r(GM_WIDTH),
            o_q + r(GLA_KEY_WIDTH), o_k + r(GLA_KEY_WIDTH), o_gv + r(GLA_WIDTH), o_gz + r(GLA_WIDTH)]
    for p in range(SWA_Q_HEADS // 2):
        a, b = o_sq + HEAD_DIM * (2 * p), o_sq + HEAD_DIM * (2 * p + 1)
        cols += [a + r(half), b + r(half), a + half + r(half), b + half + r(half)]
    for g in range(SWA_KV_HEADS):
        a = o_sk + HEAD_DIM * g
        cols += [a + r(half), a + r(half), a + half + r(half), a + half + r(half)]
    cols += [o_sv + r(SWA_KV_WIDTH)]
    cols += [_swa_out_order() + o_sz]
    cols += [o_glr + r(GLA_GATE_RANK), np.full(LANES - GLA_GATE_RANK, -1)]
    cols = np.concatenate(cols)
    assert cols.shape == (D_IN_PAD,)
    return cols


def _runs(idx):
    idx = np.asarray(idx)
    runs, start = [], 0
    for end in range(1, len(idx) + 1):
        if end < len(idx) and idx[end] == idx[end - 1] + 1 and idx[end - 1] >= 0:
            continue
        if end < len(idx) and idx[end] < 0 and idx[end - 1] < 0:
            continue
        runs.append((start, int(idx[start]), end - start))
        start = end
    return runs


def _relayout_rows_kernel(w_ref, o_ref, *, runs):
    for dst, src, n in runs:
        o_ref[dst:dst + n, :] = w_ref[src:src + n, :].astype(o_ref.dtype)


def _relayout_rows(w, idx):
    depth, r, c = w.shape
    return pl.pallas_call(
        functools.partial(_relayout_rows_kernel, runs=_runs(idx)),
        grid=(depth,),
        in_specs=[pl.BlockSpec((None, r, c), lambda l: (l, 0, 0))],
        out_specs=pl.BlockSpec((None, len(idx), c), lambda l: (l, 0, 0)),
        out_shape=jax.ShapeDtypeStruct((depth, len(idx), c), BF16),
        compiler_params=pltpu.CompilerParams(vmem_limit_bytes=VMEM_LIMIT_BYTES),
        name="weight_relayout_rows",
    )(w)


def _dot_hi_lo(a, b):
    a_hi, b_hi = a.astype(BF16), b.astype(BF16)
    a_lo = (a - a_hi.astype(F32)).astype(BF16)
    b_lo = (b - b_hi.astype(F32)).astype(BF16)
    return _dot(a_hi, b_hi) + _dot(a_hi, b_lo) + _dot(a_lo, b_hi)


def _relayout_t_kernel(wt_ref, wg2t_ref, o_ref, *, idx, block):
    for j in range(len(idx) // block):
        pieces = []
        for _, src, n in _runs(idx[j * block:(j + 1) * block]):
            pieces.append(jnp.zeros((n, wt_ref.shape[1]), F32) if src < 0 else wt_ref[src:src + n, :])
        rows_t = jnp.concatenate(pieces, axis=0)
        lo = C_GLA_GLR - j * block
        if 0 <= lo < block:
            folded = _dot_hi_lo(wg2t_ref[...], rows_t[lo:lo + LANES])
            parts = [rows_t[:lo], folded, rows_t[lo + LANES:]]
            rows_t = jnp.concatenate([p for p in parts if p.shape[0]], axis=0)
        o_ref[:, j * block:(j + 1) * block] = rows_t.T.astype(o_ref.dtype)


def _relayout_t(wt, wg2t, idx, block):
    depth, c, r = wt.shape
    assert len(idx) % block == 0
    return pl.pallas_call(
        functools.partial(_relayout_t_kernel, idx=np.asarray(idx), block=block),
        grid=(depth,),
        in_specs=[pl.BlockSpec((None, c, r), lambda l: (l, 0, 0)),
                  pl.BlockSpec((None, LANES, LANES), lambda l: (l, 0, 0))],
        out_specs=pl.BlockSpec((None, r, len(idx)), lambda l: (l, 0, 0)),
        out_shape=jax.ShapeDtypeStruct((depth, r, len(idx)), BF16),
        compiler_params=pltpu.CompilerParams(vmem_limit_bytes=VMEM_LIMIT_BYTES),
        name="weight_relayout_t",
    )(wt, wg2t)


def _silu(z):
    hz = 0.5 * z
    return hz + hz * jnp.tanh(hz)


def _dot(a, b):
    return jnp.dot(a, b, preferred_element_type=F32)


def _dot_nt(a, b):
    return lax.dot_general(a, b, (((1,), (1,)), ((), ())), preferred_element_type=F32)


def _rope_table_kernel(pos_ref, invf_ref, cos_ref, sin_ref):
    half = HEAD_DIM // 2
    ang = pos_ref[...] * invf_ref[...]
    cos_d, sin_d = jnp.cos(ang), jnp.sin(ang)
    lane = lax.broadcasted_iota(jnp.int32, ang.shape, 1)

    def spread(t, m):
        if m:
            t = pltpu.roll(t, LANES - half * m, 1)
        t = jnp.where(lane < half, t, 0.0)
        t = t + pltpu.roll(t, half, 1)
        return t + pltpu.roll(t, 2 * half, 1)

    for m in range(LANES // half):
        cos_ref[m] = spread(cos_d, m)
        s = spread(sin_d, m)
        sin_ref[m] = jnp.where(lane < LANES // 2, -s, s)


def _layer_kernel(layer_ref, sinks_ref, x_ref, cos_ref, sin_ref, pre_g_ref, w_in_ref, lng_ref, lnb_ref,
                  ws_ref, bs_ref, bg_ref, gng_ref, w_out_ref, post_g_ref,
                  o_ref, proj_ref, y_ref, state_ref, kprev_ref, vprev_ref,
                  gcum_ref, gk_ref, gv_ref, states_ref, ogla_ref, *, n_sub):
    step = pl.program_id(1)

    @pl.when(step == 0)
    def _():
        state_ref[...] = jnp.zeros_like(state_ref)
        kprev_ref[...] = jnp.zeros_like(kprev_ref)
        vprev_ref[...] = jnp.zeros_like(vprev_ref)
        ogla_ref[...] = jnp.zeros_like(ogla_ref)

    x = x_ref[...]
    h = x * lax.rsqrt(jnp.mean(x * x, axis=-1, keepdims=True) + NORM_EPS) * pre_g_ref[...]
    proj_ref[...] = _dot(h.astype(BF16), w_in_ref[...])

    def gla_finish(rows):
        o_gla = ogla_ref[...]
        seg_r = lax.broadcasted_iota(jnp.int32, (GLA_WIDTH, GLA_WIDTH), 0) // GLA_DV
        seg_c = lax.broadcasted_iota(jnp.int32, (GLA_WIDTH, GLA_WIDTH), 1) // GLA_DV
        head_ones = jnp.where(seg_r == seg_c, 1.0, 0.0).astype(BF16)
        sq = o_gla * o_gla
        sq_hi = sq.astype(BF16)
        sq_lo = (sq - sq_hi.astype(F32)).astype(BF16)
        sums = _dot(jnp.concatenate([sq_hi, sq_lo], axis=0), head_ones)
        ms = (sums[:BLK] + sums[BLK:]) * (1.0 / GLA_DV)
        yb = o_gla * lax.rsqrt(ms + NORM_EPS) * gng_ref[...]
        yb = yb * _silu(proj_ref[rows, C_GLA_Z:C_GLA_Z + GLA_WIDTH])
        y_ref[rows, Y_B:Y_B + GLA_WIDTH] = yb.astype(BF16)

    def sub_block(i, carry):
        rows = pl.ds(pl.multiple_of(i * BLK, BLK), BLK)
        row = lax.broadcasted_iota(jnp.int32, (BLK, BLK), 0)
        col = lax.broadcasted_iota(jnp.int32, (BLK, BLK), 1)
        lane = lax.broadcasted_iota(jnp.int32, (BLK, LANES), 1)
        n_chunk = BLK // GLA_CHUNK

        def seg(c0, width):
            return proj_ref[rows, c0:c0 + width]

        v = seg(C_GM_V, GM_WIDTH)
        mu = jnp.mean(v, axis=-1, keepdims=True)
        vc = v - mu
        var = jnp.mean(vc * vc, axis=-1, keepdims=True)

        logit = seg(C_GLA_GLR, LANES) + bg_ref[...]

        cosb = cos_ref[rows, :]
        sinb = sin_ref[rows, :]

        def rope(t):
            return t * cosb + pltpu.roll(t, LANES // 2, 1) * sinb

        lane2 = lax.broadcasted_iota(jnp.int32, (2 * BLK, LANES), 1)
        k_cur = [rope(seg(C_SWA_K + LANES * g, LANES)).astype(BF16) for g in range(SWA_KV_HEADS)]
        v_cur = seg(C_SWA_V, SWA_KV_WIDTH).astype(BF16)
        k_all = [jnp.concatenate([kprev_ref[g], k_cur[g]], axis=0)
                 for g in range(SWA_KV_HEADS)]
        v_all = jnp.concatenate([vprev_ref[...], v_cur], axis=0)
        one = jnp.ones((), BF16)
        v_aug = [jnp.where(lane2 < HEAD_DIM, v_all, one), jnp.where(lane2 < HEAD_DIM, one, v_all)]
        tq = lax.broadcasted_iota(jnp.int32, (BLK, 2 * BLK), 0)
        kc = lax.broadcasted_iota(jnp.int32, (BLK, 2 * BLK), 1)
        first_block = (step * n_sub + i) == 0
        prev_lo = jnp.where(first_block, BLK, 0)
        in_window = ((kc < BLK) & (kc > tq + prev_lo)) | ((kc >= BLK) & (kc - BLK <= tq))
        bias = jnp.where(in_window, 0.0, -jnp.inf)

        gla_finish(pl.ds(pl.multiple_of(jnp.maximum(i - 1, 0) * BLK, BLK), BLK))

        vln = (vc * lax.rsqrt(var + NORM_EPS) * lng_ref[...] + lnb_ref[...]).astype(BF16)
        w_tril = jnp.concatenate(
            [jnp.where(col <= row, ws_ref[hh], 0.0) for hh in range(GM_HEADS)], axis=0).astype(BF16)
        sv_all = _dot(w_tril, vln)
        sv_h = [sv_all[hh * BLK:(hh + 1) * BLK] for hh in range(GM_HEADS)]

        log_a = (jnp.minimum(logit, 0.0) - jnp.log1p(jnp.exp(-jnp.abs(logit)))) * (LOG2E / GLA_TAU)
        same_chunk = (row // GLA_CHUNK) == (col // GLA_CHUNK)
        tri16 = jnp.where(same_chunk & (col <= row), 1.0, 0.0).astype(BF16)
        ones16 = jnp.where(same_chunk, 1.0, 0.0).astype(BF16)
        la_hi = log_a.astype(BF16)
        la_lo = (log_a - la_hi.astype(F32)).astype(BF16)
        sums = _dot(jnp.concatenate([tri16, ones16], axis=0), jnp.concatenate([la_hi, la_lo], axis=1))
        g_cum = sums[:BLK, :LANES] + sums[:BLK, LANES:]
        g_end = sums[BLK:, :LANES] + sums[BLK:, LANES:]

        q_pairs = [rope(seg(C_SWA_Q + LANES * p, LANES)) * (HEAD_DIM ** -0.5 * LOG2E)
                   for p in range(SWA_Q_HEADS // 2)]
        lane_parity = (lane // (HEAD_DIM // 2)) % 2

        q = seg(C_GLA_Q, GLA_KEY_WIDTH) * (GLA_DK ** -0.5)
        k = seg(C_GLA_K, GLA_KEY_WIDTH)
        gv = seg(C_GLA_V, GLA_WIDTH)
        gcum_ref[...] = g_cum
        gk_ref[...] = k
        for t in range(GLA_WIDTH // LANES):
            gv_ref[t] = gv[:, t * LANES:(t + 1) * LANES]
        hk = lax.broadcasted_iota(jnp.int32, (GLA_KEY_WIDTH, GLA_WIDTH), 0) // GLA_DK
        hv = lax.broadcasted_iota(jnp.int32, (GLA_KEY_WIDTH, GLA_WIDTH), 1) // GLA_DV
        head_match = hk == hv
        expand = jnp.where(head_match, 1.0, 0.0).astype(BF16)

        def state_updates():
            k_dec = k * jnp.exp2(g_end - g_cum)
            k_dec_t = k_dec.T
            lhs = jnp.concatenate(
                [jnp.where(col // GLA_CHUNK == jj, k_dec_t, 0.0) for jj in range(n_chunk)], axis=0)
            return _dot(lhs.astype(BF16), gv.astype(BF16))

        upd = state_updates()
        q_dec = q * jnp.exp2(g_cum)
        a_t = jnp.exp2(g_end).T

        t_local = row % GLA_CHUNK

        def chunk_rows(ref, s, n):
            tiles = [ref] if len(ref.shape) == 2 else [ref.at[t] for t in range(ref.shape[0])]
            return jnp.concatenate(
                [jnp.concatenate([t[pl.ds(c * GLA_CHUNK + s, n, stride=0), :] for c in range(n_chunk)], axis=0)
                 for t in tiles], axis=1)

        def upper_half(t):
            return t.reshape(n_chunk, 2, SUBLANES, t.shape[1])[:, 1].reshape(BLK // 2, t.shape[1])

        q_up, g_up, t_up = upper_half(q), upper_half(g_cum), upper_half(t_local)

        def intra_terms(s_list):
            upper = s_list[0] >= SUBLANES
            qq, gg, tt, n = (q_up, g_up, t_up, SUBLANES) if upper else (q, g_cum, t_local, GLA_CHUNK)
            terms = []
            for s in s_list:
                term = qq * chunk_rows(gk_ref, s, n) * jnp.exp2(gg - chunk_rows(gcum_ref, s, n))
                terms.append(jnp.where(tt >= s, term, 0.0).astype(BF16))
            spread = _dot(jnp.concatenate(terms, axis=0), expand)
            m = qq.shape[0]
            out = spread[0:m] * chunk_rows(gv_ref, s_list[0], n)
            for i, s in enumerate(s_list[1:], 1):
                out = out + spread[i * m:(i + 1) * m] * chunk_rows(gv_ref, s, n)
            return out

        def attention_group(g):
            qm = []
            for hh in range(g * SWA_GROUP, (g + 1) * SWA_GROUP):
                p, par = divmod(hh, 2)
                qm.append(jnp.where(lane_parity == par, q_pairs[p], 0.0).astype(BF16))
            sc = _dot_nt(jnp.concatenate(qm, axis=0), k_all[g])
            es, sinks_w = [], []
            for i, hh in enumerate(range(g * SWA_GROUP, (g + 1) * SWA_GROUP)):
                sc_h = sc[i * BLK:(i + 1) * BLK] + bias
                sink = sinks_ref[layer_ref[0], hh] * LOG2E
                m = jnp.maximum(jnp.max(sc_h, axis=-1, keepdims=True), sink)
                es.append(jnp.exp2(sc_h - m).astype(BF16))
                sinks_w.append(jnp.exp2(sink - m))
            pv_all = _dot(jnp.concatenate(es, axis=0), v_aug[g])
            return [pv_all[i * BLK:(i + 1) * BLK] for i in range(SWA_GROUP)], sinks_w

        def state_pass():
            state = state_ref[...]
            for jj in range(n_chunk):
                states_ref[jj * BLK:(jj + 1) * BLK, :] = state.astype(BF16)
                a_j = a_t[:, jj * GLA_CHUNK:jj * GLA_CHUNK + 1]
                state = a_j * state + jnp.where(head_match, upd[jj * BLK:(jj + 1) * BLK], 0.0)
            state_ref[...] = state
            q_blk = jnp.concatenate(
                [jnp.where(row // GLA_CHUNK == jj, q_dec, 0.0) for jj in range(n_chunk)], axis=1)
            return _dot(q_blk.astype(BF16), states_ref[...])

        half = SUBLANES // 2
        key_rows = {"L0": range(0, half), "L1": range(half, SUBLANES),
                    "U0": range(SUBLANES, SUBLANES + half), "U1": range(SUBLANES + half, GLA_CHUNK)}
        o_lo = o_up = o_inter = None
        pv, sink_w = [None] * SWA_Q_HEADS, [None] * SWA_Q_HEADS
        for stage in MIXER_STAGE_ORDER:
            if stage[0] == "L":
                term = intra_terms(list(key_rows[stage]))
                o_lo = term if o_lo is None else o_lo + term
            elif stage[0] == "U":
                term = intra_terms(list(key_rows[stage]))
                o_up = term if o_up is None else o_up + term
            elif stage[0] == "G":
                g = int(stage[1])
                pv[g * SWA_GROUP:(g + 1) * SWA_GROUP], sink_w[g * SWA_GROUP:(g + 1) * SWA_GROUP] = attention_group(g)
            else:
                o_inter = state_pass()

        o_up = jnp.concatenate(
            [jnp.zeros((n_chunk, 1, SUBLANES, GLA_WIDTH), F32),
             o_up.reshape(n_chunk, 1, SUBLANES, GLA_WIDTH)], axis=1).reshape(BLK, GLA_WIDTH)
        ogla_ref[...] = o_lo + o_up + o_inter

        for c in range(SWA_GROUP):
            lo_head = lane < HEAD_DIM
            num = jnp.where(lo_head, pv[c], pv[SWA_GROUP + c])
            den = pltpu.roll(jnp.where(lo_head, pv[SWA_GROUP + c], pv[c]), LANES // 2, 1)
            den = den + jnp.where(lo_head, sink_w[c], sink_w[SWA_GROUP + c])
            yc = num / den * _silu(seg(C_SWA_Z + LANES * c, LANES))
            y_ref[rows, Y_C + LANES * c:Y_C + LANES * (c + 1)] = yc.astype(BF16)
        for g in range(SWA_KV_HEADS):
            kprev_ref[g] = k_cur[g]
        vprev_ref[...] = v_cur

        lane_head = lax.broadcasted_iota(jnp.int32, (BLK, GM_WIDTH), 1) // HEAD_DIM
        sv = sv_h[0]
        for hh in range(1, GM_HEADS):
            sv = jnp.where(lane_head == hh, sv_h[hh], sv)
        ya = seg(C_GM_U, GM_WIDTH) * (sv + bs_ref[...]) * _silu(seg(C_GM_Z, GM_WIDTH))
        y_ref[rows, Y_A:Y_A + GM_WIDTH] = ya.astype(BF16)
        return carry

    lax.fori_loop(0, n_sub, sub_block, 0)
    gla_finish(pl.ds((n_sub - 1) * BLK, BLK))

    y2 = _dot(y_ref[...], w_out_ref[...])
    y2 = y2 * lax.rsqrt(jnp.mean(y2 * y2, axis=-1, keepdims=True) + NORM_EPS) * post_g_ref[...]
    o_ref[...] = x_ref[...] + y2


def _rope_tables(positions):
    n = positions.size
    half = HEAD_DIM // 2
    inv_freq = jnp.power(ROPE_THETA, -jnp.arange(half, dtype=F32) * 2.0 / HEAD_DIM)
    per_row = LANES // half
    quarter = n // per_row
    invf = jnp.tile(inv_freq, per_row).reshape(1, LANES)
    pos = jnp.repeat(positions.reshape(per_row, quarter).T.astype(F32), half, axis=1)
    rows = ROPE_ROWS_PER_STEP
    assert quarter % rows == 0
    out_spec = pl.BlockSpec((per_row, rows, LANES), lambda i: (0, i, 0))
    cos_t, sin_t = pl.pallas_call(
        _rope_table_kernel,
        grid=(quarter // rows,),
        in_specs=[pl.BlockSpec((rows, LANES), lambda i: (i, 0)), pl.BlockSpec((1, LANES), lambda i: (0, 0))],
        out_specs=[out_spec, out_spec],
        out_shape=[jax.ShapeDtypeStruct((per_row, quarter, LANES), F32)] * 2,
        name="rope_tables",
    )(pos, invf)
    return cos_t.reshape(n, LANES), sin_t.reshape(n, LANES)


def _layer_call(batch, seq):
    tb = TOKEN_BLOCK
    n_sub = tb // BLK
    steps = seq // tb
    tok = lambda b, j, l: (b * steps + j, 0)
    layer3 = lambda b, j, l: (l[0], 0, 0)
    layer4 = lambda b, j, l: (l[0], 0, 0, 0)
    in_specs = [
        pl.BlockSpec(memory_space=pltpu.SMEM),
        pl.BlockSpec((tb, D_MODEL), tok),
        pl.BlockSpec((tb, LANES), tok),
        pl.BlockSpec((tb, LANES), tok),
        pl.BlockSpec((None, 1, D_MODEL), layer3),
        pl.BlockSpec((None, D_MODEL, D_IN_PAD), layer3),
        pl.BlockSpec((None, 1, GM_WIDTH), layer3),
        pl.BlockSpec((None, 1, GM_WIDTH), layer3),
        pl.BlockSpec((None, GM_HEADS, BLK, BLK), layer4),
        pl.BlockSpec((None, BLK, GM_WIDTH), layer3),
        pl.BlockSpec((None, 1, GLA_KEY_WIDTH), layer3),
        pl.BlockSpec((None, 1, GLA_WIDTH), layer3),
        pl.BlockSpec((None, D_MIX, D_MODEL), layer3),
        pl.BlockSpec((None, 1, D_MODEL), layer3),
    ]
    return pl.pallas_call(
        functools.partial(_layer_kernel, n_sub=n_sub),
        grid_spec=pltpu.PrefetchScalarGridSpec(
            num_scalar_prefetch=1,
            grid=(batch, steps),
            in_specs=in_specs,
            out_specs=pl.BlockSpec((tb, D_MODEL), tok),
            scratch_shapes=[
                pltpu.VMEM((tb, D_IN_PAD), F32),
                pltpu.VMEM((tb, D_MIX), BF16),
                pltpu.VMEM((GLA_KEY_WIDTH, GLA_WIDTH), F32),
                pltpu.VMEM((SWA_KV_HEADS, BLK, LANES), BF16),
                pltpu.VMEM((BLK, SWA_KV_WIDTH), BF16),
                pltpu.VMEM((BLK, GLA_KEY_WIDTH), F32),
                pltpu.VMEM((BLK, GLA_KEY_WIDTH), F32),
                pltpu.VMEM((GLA_WIDTH // LANES, BLK, LANES), F32),
                pltpu.VMEM((BLK // GLA_CHUNK * GLA_KEY_WIDTH, GLA_WIDTH), BF16),
                pltpu.VMEM((BLK, GLA_WIDTH), F32),
            ]),
        out_shape=jax.ShapeDtypeStruct((batch * seq, D_MODEL), F32),
        compiler_params=pltpu.CompilerParams(
            dimension_semantics=("arbitrary", "arbitrary"),
            vmem_limit_bytes=VMEM_LIMIT_BYTES),
        name="hybrid_layer",
    )


def kernel(x, positions, pre_g, w_in, gm_ln_g, gm_ln_b, gm_ws, gm_bs, gla_wg2, gla_bg, gla_norm_g,
           swa_sinks, w_out, post_g):
    batch, seq, d_model = x.shape
    assert d_model == D_MODEL and seq % TOKEN_BLOCK == 0
    assert w_in.shape == (DEPTH, D_MODEL, D_IN) and w_out.shape == (DEPTH, D_MIX, D_MODEL)

    row = lambda p: p[:, None, :]
    wg2t = jnp.pad(jnp.swapaxes(gla_wg2, 1, 2), ((0, 0), (0, 0), (0, LANES - GLA_GATE_RANK)))
    w_in_p = _relayout_t(jnp.swapaxes(w_in, 1, 2), wg2t, _in_proj_columns(), block=2 * LANES)
    out_rows = np.concatenate([np.arange(Y_C), Y_C + _swa_out_order()])
    w_out_p = _relayout_rows(w_out, out_rows)
    bs_p = jnp.repeat(jnp.swapaxes(gm_bs, 1, 2), HEAD_DIM, axis=2)
    gng_p = jnp.tile(gla_norm_g, (1, GLA_HEADS))

    cos_t, sin_t = _rope_tables(positions)
    layer = _layer_call(batch, seq)
    h = x.reshape(batch * seq, D_MODEL)
    for l in range(DEPTH):
        h = layer(jnp.full((1,), l, jnp.int32), swa_sinks, h, cos_t, sin_t, row(pre_g), w_in_p,
                  row(gm_ln_g), row(gm_ln_b), gm_ws, bs_p, row(gla_bg), row(gng_p),
                  w_out_p, row(post_g))
    return h.reshape(batch, seq, D_MODEL)
```
